```python
import math
import numpy as np
import jax
import jax.numpy as jnp
from jax import lax

D_MODEL = 1024
BATCH = 4
SEQ = 8192
DEPTH = 2

GRID_W = 64
CTX_LEN = 256
N_EVEN = (DEPTH + 1) // 2
N_ODD = DEPTH // 2
EPS = 1e-6
CHUNK = 64

HG_HEADS = 4
HG_DK = 128
HG_DV = 128
HG_WIDTH = HG_HEADS * HG_DV
HG_COLS = 5 * HG_WIDTH

RW_HEADS = 8
RW_DH = 64
RW_WIDTH = RW_HEADS * RW_DH
RW_DECAY_LORA = 64
RW_AAA_LORA = 64
RW_GATE_LORA = 128
RW_GN_EPS = 64e-5
RW_SPLITS = (RW_WIDTH, RW_WIDTH, RW_WIDTH, RW_DECAY_LORA, RW_DECAY_LORA, RW_AAA_LORA, RW_AAA_LORA, RW_GATE_LORA)
RW_COLS = sum(RW_SPLITS)
EVEN_COLS = HG_COLS + RW_COLS

GLA_HEADS = 4
GLA_DK = 64
GLA_DV = 128
GLA_WIDTH = GLA_HEADS * GLA_DV
GLA_GATE_LORA = 16
GLA_TAU = 16.0

S5_WIDTH = 512
S5_GROUP = 16
S5_GROUPS = S5_WIDTH // S5_GROUP
S5_STATE = 64
ODD_SPLITS = (GLA_HEADS * GLA_DK, GLA_HEADS * GLA_DK, GLA_WIDTH, GLA_WIDTH, GLA_GATE_LORA, GLA_GATE_LORA, S5_WIDTH)
ODD_COLS = sum(ODD_SPLITS)

D_FF = 2816

kernel_name = 'hybrid_hgrn2_rwkv7_gla_s5_prefix_dit'


def split_cols(p, sizes):
    idx = [int(i) for i in np.cumsum(sizes)[:-1]]
    return jnp.split(p, idx, axis=-1)


def heads(z, n):
    return z.reshape(z.shape[:-1] + (n, z.shape[-1] // n))


def rms_norm(z, g):
    z32 = z.astype(jnp.float32)
    return z32 * lax.rsqrt(jnp.mean(z32 * z32, axis=-1, keepdims=True) + EPS) * g.astype(jnp.float32)


def stack_dirs(fwd, bwd):
    return jnp.concatenate([fwd, jnp.flip(bwd, axis=1)], axis=0)


def merge_dirs(o):
    n = o.shape[0] // 2
    return o[:n] + jnp.flip(o[n:], axis=1)


def centred_shift(p, mu):
    prev = jnp.pad(p[:, :-1], ((0, 0), (1, 0), (0, 0)))
    nxt = jnp.pad(p[:, 1:], ((0, 0), (0, 1), (0, 0)))
    return p + mu[0] * (prev - p) + mu[1] * (nxt - p)


def chunk_gated_linear_scan(q, k, v, log_f, s0):
    n, t, h, _ = q.shape
    dv = v.shape[-1]
    nc = t // CHUNK

    def to_chunks(z):
        return jnp.moveaxis(z.reshape(n, nc, CHUNK, h, z.shape[-1]), 1, 0)

    mask = jnp.tril(jnp.ones((CHUNK, CHUNK), dtype=bool))[None, :, :, None, None]

    def step(s, inp):
        qc, kc, vc, gc = inp
        b = jnp.cumsum(gc, axis=1)
        decay = jnp.exp(jnp.where(mask, b[:, :, None] - b[:, None, :], -jnp.inf))
        att = jnp.einsum('ntshk,nshk->nths', qc[:, :, None] * decay, kc)
        o = jnp.einsum('nths,nshv->nthv', att, vc) + jnp.einsum('nthk,nhkv->nthv', qc * jnp.exp(b), s)
        b_last = b[:, -1]
        s_new = jnp.exp(b_last)[..., None] * s + jnp.einsum(
            'nshk,nshv->nhkv', kc * jnp.exp(b_last[:, None] - b), vc)
        return s_new, o

    s_fin, o = lax.scan(step, s0, (to_chunks(q), to_chunks(k), to_chunks(v), to_chunks(log_f)))
    return jnp.moveaxis(o, 0, 1).reshape(n, t, h, dv), s_fin


def rwkv7_scan(r, w, k, v, a, b, s0):
    def step(s, inp):
        r_t, w_t, k_t, v_t, a_t, b_t = inp
        sa = jnp.einsum('nhvk,nhk->nhv', s, a_t)
        s = s * w_t[:, :, None, :] + sa[..., None] * b_t[:, :, None, :] + v_t[..., None] * k_t[:, :, None, :]
        return s, jnp.einsum('nhvk,nhk->nhv', s, r_t)

    xs = tuple(jnp.swapaxes(z, 0, 1) for z in (r, w, k, v, a, b))
    s_fin, o = lax.scan(step, s0, xs)
    return jnp.swapaxes(o, 0, 1), s_fin


def _ssm_combine(e1, e2):
    a1, b1 = e1
    a2, b2 = e2
    return a1 * a2, a2 * b1 + b2


def s5_scan(u, a_re, a_im, log_dt, b_re, b_im, h0, reverse):
    f32 = jnp.float32
    lam = lax.complex(a_re.astype(f32), a_im.astype(f32))
    lam_bar = jnp.exp(lam * jnp.exp(log_dt.astype(f32))[:, None])
    b_bar = ((lam_bar - 1.0) / lam)[..., None] * lax.complex(b_re.astype(f32), b_im.astype(f32))
    bu = lax.complex(jnp.einsum('gnc,btgc->btgn', b_bar.real, u), jnp.einsum('gnc,btgc->btgn', b_bar.imag, u))
    first, last = (-1, 0) if reverse else (0, -1)
    bu = bu.at[:, first].add(lam_bar * h0)
    _, hs = lax.associative_scan(_ssm_combine, (jnp.broadcast_to(lam_bar, bu.shape), bu), reverse=reverse, axis=1)
    return hs, hs[:, last]


def s5_readout(hs, c_re, c_im):
    f32 = jnp.float32
    return jnp.einsum('gcn,btgn->btgc', c_re.astype(f32), hs.real) - jnp.einsum('gcn,btgn->btgc', c_im.astype(f32), hs.imag)


def even_mix(h, s0, prm, with_out):
    (w_in, w_out, lb, hg_g, mu, w0, w2, a0, a2, g2, k_k, k_a, r_k, ln_g, ln_b) = prm
    f32 = jnp.float32
    bsz, t, _ = h.shape
    p = (h @ w_in).astype(f32)
    q, i, f_fw, f_bw, g = jnp.split(p[..., :HG_COLS], 5, axis=-1)
    r, k, v, wd_fw, wd_bw, ad_fw, ad_bw, gd = split_cols(centred_shift(p[..., HG_COLS:], mu.astype(f32)), RW_SPLITS)
    s_hg0, s_rw0 = s0

    lb = lb.astype(f32)
    f_st = stack_dirs(lb[0] + (1.0 - lb[0]) * jax.nn.sigmoid(f_fw), lb[1] + (1.0 - lb[1]) * jax.nn.sigmoid(f_bw))
    q_hg = jax.nn.silu(q) * HG_DK ** -0.5
    hg_o, s_hg = chunk_gated_linear_scan(
        heads(stack_dirs(q_hg, q_hg), HG_HEADS), heads(1.0 - f_st, HG_HEADS),
        heads(stack_dirs(i, i), HG_HEADS), heads(jnp.log(f_st), HG_HEADS), s_hg0)

    kk = heads(k * k_k, RW_HEADS)
    kk = kk / jnp.maximum(jnp.sqrt(jnp.sum(kk * kk, axis=-1, keepdims=True)), 1e-12)
    k_h = heads(k, RW_HEADS)
    decay, k_dir, kka = [], [], []
    for d, (wd, ad) in enumerate(((wd_fw, ad_fw), (wd_bw, ad_bw))):
        w_log = -jax.nn.softplus(-(w0[d] + jnp.tanh(wd) @ w2[d])) - 0.5
        a = heads(jax.nn.sigmoid(a0[d] + ad @ a2[d]), RW_HEADS)
        decay.append(heads(jnp.exp(-jnp.exp(w_log)), RW_HEADS))
        k_dir.append(k_h * (1.0 + (a - 1.0) * heads(k_a, RW_HEADS)))
        kka.append(kk * a)
    r_h, v_h = heads(r, RW_HEADS), heads(v, RW_HEADS)
    rw_o, s_rw = rwkv7_scan(stack_dirs(r_h, r_h), stack_dirs(decay[0], decay[1]), stack_dirs(k_dir[0], k_dir[1]),
                            stack_dirs(v_h, v_h), stack_dirs(-kk, -kk), stack_dirs(kka[0], kka[1]), s_rw0)
    states = (s_hg, s_rw)
    if not with_out:
        return None, states

    o_hg = rms_norm(merge_dirs(hg_o), hg_g) * jax.nn.silu(heads(g, HG_HEADS))
    o_rw = merge_dirs(rw_o)
    mean = jnp.mean(o_rw, axis=-1, keepdims=True)
    var = jnp.mean(jnp.square(o_rw - mean), axis=-1, keepdims=True)
    o_rw = (o_rw - mean) * lax.rsqrt(var + RW_GN_EPS) * heads(ln_g, RW_HEADS) + heads(ln_b, RW_HEADS)
    bonus = jnp.sum(r_h * (k_dir[0] + k_dir[1]) * r_k, axis=-1, keepdims=True) * v_h
    o_rw = (o_rw + bonus).reshape(bsz, t, RW_WIDTH) * (jax.nn.sigmoid(gd) @ g2)
    mixed = jnp.concatenate([o_hg.reshape(bsz, t, HG_WIDTH), o_rw], axis=-1).astype(h.dtype)
    return mixed @ w_out, states


def odd_mix(h, s0, prm, with_out):
    (w_in, w_out, gla_w2, gla_b, gla_g, a_re, a_im, log_dt, b_re, b_im, c_re, c_im, d_skip, w_glu, b_glu) = prm
    f32 = jnp.float32
    bsz, t, _ = h.shape
    p = (h @ w_in).astype(f32)
    q, k, v, g, gd_fw, gd_bw, u = split_cols(p, ODD_SPLITS)
    s_gla0, s_s50 = s0

    log_a = [jax.nn.log_sigmoid(gd @ gla_w2[d] + gla_b[d]) / GLA_TAU for d, gd in enumerate((gd_fw, gd_bw))]
    q_s = q * GLA_DK ** -0.5
    gla_o, s_gla = chunk_gated_linear_scan(
        heads(stack_dirs(q_s, q_s), GLA_HEADS), heads(stack_dirs(k, k), GLA_HEADS),
        heads(stack_dirs(v, v), GLA_HEADS), heads(stack_dirs(log_a[0], log_a[1]), GLA_HEADS), s_gla0)

    u_g = heads(u, S5_GROUPS)
    s5_fin, readouts = [], []
    for d in range(2):
        hs, h_fin = s5_scan(u_g, a_re[d], a_im[d], log_dt[d], b_re[d], b_im[d], s_s50[d], reverse=(d == 1))
        s5_fin.append(h_fin)
        if with_out:
            readouts.append(s5_readout(hs, c_re[d], c_im[d]))
    states = (s_gla, (s5_fin[0], s5_fin[1]))
    if not with_out:
        return None, states

    o_gla = rms_norm(merge_dirs(gla_o), gla_g) * jax.nn.silu(heads(g, GLA_HEADS))
    y = (readouts[0] + readouts[1]).reshape(bsz, t, S5_WIDTH) + d_skip * u
    z = jax.nn.gelu(y)
    o_s5 = z * jax.nn.sigmoid(z @ w_glu + b_glu)
    mixed = jnp.concatenate([o_gla.reshape(bsz, t, GLA_WIDTH), o_s5], axis=-1).astype(h.dtype)
    return mixed @ w_out, states


def conv_ffn(h, w_up, conv_w, conv_b, w_down, rows, cols):
    bsz, t, _ = h.shape
    gate, val = jnp.split(h @ w_up, 2, axis=-1)
    grid = jnp.pad(gate.reshape(bsz, rows, cols, D_FF), ((0, 0), (1, 1), (1, 1), (0, 0)))
    conv = conv_b + sum(grid[:, di:di + rows, dj:dj + cols] * conv_w[di, dj] for di in range(3) for dj in range(3))
    return (jax.nn.silu(conv) * val.reshape(bsz, rows, cols, D_FF)).reshape(bsz, t, D_FF) @ w_down


def setup_inputs(seed: int = 0) -> dict:
    key = jax.random.key(seed)
    keys = iter(jax.random.split(key, 48))
    f32 = jnp.float32

    def nrm(shape, scale):
        return scale * jax.random.normal(next(keys), shape, f32)

    def unif(shape, lo, hi):
        return jax.random.uniform(next(keys), shape, f32, lo, hi)

    d = D_MODEL
    return {
        'x': nrm((BATCH, SEQ, d), 1.0),
        'c': nrm((BATCH, d), 1.0),
        'ctx': nrm((BATCH, CTX_LEN, d), 1.0),
        'c_ctx': nrm((d,), 1.0),
        'ada_w': nrm((DEPTH, d, 6 * d), 0.5 * d ** -0.5),
        'ada_b': nrm((DEPTH, 6 * d), 0.02),
        'norm1_g': 1.0 + nrm((DEPTH, d), 0.02),
        'norm2_g': 1.0 + nrm((DEPTH, d), 0.02),
        'final_g': 1.0 + nrm((d,), 0.02),
        'ev_w_in': nrm((N_EVEN, d, EVEN_COLS), d ** -0.5),
        'ev_w_out': nrm((N_EVEN, HG_WIDTH + RW_WIDTH, d), (HG_WIDTH + RW_WIDTH) ** -0.5),
        'hg_lb': nrm((N_EVEN + 1, 2, HG_HEADS * HG_DK), 0.5),
        'hg_norm_g': 1.0 + nrm((N_EVEN, HG_DV), 0.02),
        'rw_mu': unif((N_EVEN, 2, RW_COLS), 0.0, 0.5),
        'rw_w0': jnp.linspace(-6.5, -1.5, RW_WIDTH, dtype=f32) + nrm((N_EVEN, 2, RW_WIDTH), 0.1),
        'rw_w2': nrm((N_EVEN, 2, RW_DECAY_LORA, RW_WIDTH), 0.5 * RW_DECAY_LORA ** -0.5),
        'rw_a0': nrm((N_EVEN, 2, RW_WIDTH), 0.1),
        'rw_a2': nrm((N_EVEN, 2, RW_AAA_LORA, RW_WIDTH), 0.5 * RW_AAA_LORA ** -0.5),
        'rw_g2': nrm((N_EVEN, RW_GATE_LORA, RW_WIDTH), RW_GATE_LORA ** -0.5),
        'rw_k_k': 0.85 + nrm((N_EVEN, RW_WIDTH), 0.02),
        'rw_k_a': 1.0 + nrm((N_EVEN, RW_WIDTH), 0.02),
        'rw_r_k': nrm((N_EVEN, RW_HEADS, RW_DH), 0.1),
        'rw_ln_g': 1.0 + nrm((N_EVEN, RW_WIDTH), 0.02),
        'rw_ln_b': nrm((N_EVEN, RW_WIDTH), 0.02),
        'od_w_in': nrm((N_ODD, d, ODD_COLS), d ** -0.5),
        'od_w_out': nrm((N_ODD, GLA_WIDTH + S5_WIDTH, d), (GLA_WIDTH + S5_WIDTH) ** -0.5),
        'gla_w2': nrm((N_ODD, 2, GLA_GATE_LORA, GLA_HEADS * GLA_DK), GLA_GATE_LORA ** -0.5),
        'gla_b': nrm((N_ODD, 2, GLA_HEADS * GLA_DK), 0.5),
        'gla_norm_g': 1.0 + nrm((N_ODD, GLA_DV), 0.02),
        's5_a_re': -0.5 + nrm((N_ODD, 2, S5_GROUPS, S5_STATE), 0.01),
        's5_a_im': math.pi * jnp.arange(S5_STATE, dtype=f32) + nrm((N_ODD, 2, S5_GROUPS, S5_STATE), 0.01),
        's5_log_dt': unif((N_ODD, 2, S5_GROUPS), math.log(1e-3), math.log(1e-1)),
        's5_b_re': nrm((N_ODD, 2, S5_GROUPS, S5_STATE, S5_GROUP), (2 * S5_GROUP) ** -0.5),
        's5_b_im': nrm((N_ODD, 2, S5_GROUPS, S5_STATE, S5_GROUP), (2 * S5_GROUP) ** -0.5),
        's5_c_re': nrm((N_ODD, 2, S5_GROUPS, S5_GROUP, S5_STATE), S5_STATE ** -0.5),
        's5_c_im': nrm((N_ODD, 2, S5_GROUPS, S5_GROUP, S5_STATE), S5_STATE ** -0.5),
        's5_d': nrm((N_ODD, S5_WIDTH), 1.0),
        's5_w_glu': nrm((N_ODD, S5_WIDTH, S5_WIDTH), S5_WIDTH ** -0.5),
        's5_b_glu': nrm((N_ODD, S5_WIDTH), 0.02),
        'ffn_w_up': nrm((DEPTH, d, 2 * D_FF), d ** -0.5),
        'ffn_conv_w': nrm((DEPTH, 3, 3, D_FF), 1.0 / 3.0),
        'ffn_conv_b': nrm((DEPTH, D_FF), 0.02),
        'ffn_w_down': nrm((DEPTH, D_FF, d), D_FF ** -0.5),
    }


def reference(x, c, ctx, c_ctx, ada_w, ada_b, norm1_g, norm2_g, final_g,
              ev_w_in, ev_w_out, hg_lb, hg_norm_g, rw_mu, rw_w0, rw_w2, rw_a0, rw_a2, rw_g2,
              rw_k_k, rw_k_a, rw_r_k, rw_ln_g, rw_ln_b,
              od_w_in, od_w_out, gla_w2, gla_b, gla_norm_g, s5_a_re, s5_a_im, s5_log_dt,
              s5_b_re, s5_b_im, s5_c_re, s5_c_im, s5_d, s5_w_glu, s5_b_glu,
              ffn_w_up, ffn_conv_w, ffn_conv_b, ffn_w_down):
    f32 = jnp.float32
    bsz, seq, _ = x.shape
    rows = seq // GRID_W
    ctx_len = ctx.shape[1]
    lb_all = jnp.cumsum(jax.nn.softmax(hg_lb.astype(f32), axis=0), axis=0)

    for layer in range(DEPTH):
        last = layer == DEPTH - 1
        j = layer // 2
        mod = jnp.split(jax.nn.silu(c) @ ada_w[layer] + ada_b[layer], 6, axis=-1)
        mod_c = jnp.split(jax.nn.silu(c_ctx) @ ada_w[layer] + ada_b[layer], 6, axis=-1)
        sh1, sc1, g1, sh2, sc2, g2 = [m[:, None] for m in mod]
        csh1, csc1, cg1, csh2, csc2, cg2 = mod_c
        hx = (rms_norm(x, norm1_g[layer]) * (1.0 + sc1) + sh1).astype(x.dtype)
        hc = (rms_norm(ctx, norm1_g[layer]) * (1.0 + csc1) + csh1).astype(ctx.dtype)

        if layer % 2 == 0:
            prm = (ev_w_in[j], ev_w_out[j], lb_all[j], hg_norm_g[j], rw_mu[j], rw_w0[j], rw_w2[j], rw_a0[j],
                   rw_a2[j], rw_g2[j], rw_k_k[j], rw_k_a[j], rw_r_k[j], rw_ln_g[j], rw_ln_b[j])
            zero = (jnp.zeros((2 * bsz, HG_HEADS, HG_DK, HG_DV), f32),
                    jnp.zeros((2 * bsz, RW_HEADS, RW_DH, RW_DH), f32))
            ctx_mix, ctx_state = even_mix(hc, zero, prm, not last)
            x_mix, _ = even_mix(hx, ctx_state, prm, True)
        else:
            prm = (od_w_in[j], od_w_out[j], gla_w2[j], gla_b[j], gla_norm_g[j], s5_a_re[j], s5_a_im[j],
                   s5_log_dt[j], s5_b_re[j], s5_b_im[j], s5_c_re[j], s5_c_im[j], s5_d[j], s5_w_glu[j], s5_b_glu[j])
            zc = jnp.zeros((bsz, S5_GROUPS, S5_STATE), jnp.complex64)
            zero = (jnp.zeros((2 * bsz, GLA_HEADS, GLA_DK, GLA_DV), f32), (zc, zc))
            ctx_mix, ctx_state = odd_mix(hc, zero, prm, not last)
            x_mix, _ = odd_mix(hx, ctx_state, prm, True)

        x = x + g1 * x_mix
        hx2 = (rms_norm(x, norm2_g[layer]) * (1.0 + sc2) + sh2).astype(x.dtype)
        x = x + g2 * conv_ffn(hx2, ffn_w_up[layer], ffn_conv_w[layer], ffn_conv_b[layer], ffn_w_down[layer], rows, GRID_W)
        if not last:
            ctx = ctx + cg1 * ctx_mix
            hc2 = (rms_norm(ctx, norm2_g[layer]) * (1.0 + csc2) + csh2).astype(ctx.dtype)
            ctx = ctx + cg2 * conv_ffn(hc2, ffn_w_up[layer], ffn_conv_w[layer], ffn_conv_b[layer], ffn_w_down[layer], 1, ctx_len)

    return rms_norm(x, final_g).astype(x.dtype)
```

```python
import functools
import math

import numpy as np
import jax
import jax.numpy as jnp
from jax import lax
from jax.experimental import pallas as pl
from jax.experimental.pallas import tpu as pltpu

F32 = jnp.float32
BF16 = jnp.bfloat16

EPS = 1e-6
GRID_W = 64
CHUNK = 64
LEVELS = 6

HG_HEADS, HG_DK, HG_DV = 4, 128, 128
HG_WIDTH = HG_HEADS * HG_DV
RW_HEADS, RW_DH = 8, 64
RW_WIDTH = RW_HEADS * RW_DH
RW_DECAY_LORA, RW_AAA_LORA, RW_GATE_LORA = 64, 64, 128
RW_GN_EPS = 64e-5
RW_COLS = 3 * RW_WIDTH + 2 * RW_DECAY_LORA + 2 * RW_AAA_LORA + RW_GATE_LORA
GLA_HEADS, GLA_DK, GLA_DV = 4, 64, 128
GLA_WIDTH = GLA_HEADS * GLA_DV
GLA_QK = GLA_HEADS * GLA_DK
GLA_GATE_LORA = 16
GLA_TAU = 16.0
S5_WIDTH, S5_GROUP, S5_STATE = 512, 16, 64
S5_GROUPS = S5_WIDTH // S5_GROUP
S5_CHUNK = 16
S5_FEAT = S5_CHUNK * S5_GROUP
D_FF = 2816

LANES = 128
VMEM_LIMIT = 56 * 1024 * 1024

NN = ((1,), (0,))
NT = ((1,), (1,))
TN = ((0,), (0,))


def _dot(a, b, dims=NN):
    return lax.dot_general(a, b, (dims, ((), ())), preferred_element_type=F32)


def _split(a):
    hi = a.astype(BF16)
    return hi, (a - hi.astype(F32)).astype(BF16)


def mm1(a, b, dims=NN):
    return _dot(a.astype(BF16), b.astype(BF16), dims)


def mm3(a, b, dims=NN):
    ah, al = _split(a)
    bh, bl = _split(b)
    return _dot(ah, bh, dims) + (_dot(al, bh, dims) + _dot(ah, bl, dims))


def mmc(c, x, dims=NN):
    xh, xl = _split(x)
    return _dot(c, xh, dims) + _dot(c, xl, dims)


def mmcr(x, c, dims=NN):
    xh, xl = _split(x)
    return _dot(xh, c, dims) + _dot(xl, c, dims)


def _silu(x):
    return x * jax.nn.sigmoid(x)


def _params(sem):
    return pltpu.CompilerParams(dimension_semantics=sem, vmem_limit_bytes=VMEM_LIMIT)


def _full(shape):
    nd = len(shape)
    return pl.BlockSpec(shape, lambda *_: (0,) * nd)


def _tau(rev):
    t = np.arange(CHUNK)
    return (CHUNK - 1 - t) if rev else t


def _level_masks(rev):
    tau = _tau(rev)
    ti, si = tau[:, None], tau[None, :]
    out = []
    for l in range(LEVELS):
        m = (((ti >> l) & 1) == 1) & (((si >> l) & 1) == 0) & ((ti >> (l + 1)) == (si >> (l + 1)))
        out.append(m)
    return np.stack(out).astype(np.float32)


def _gls_consts(rev):
    tau = _tau(rev)
    ti, ii = tau[:, None], tau[None, :]
    mats = [ii <= ti, ii > ti]
    sel = []
    for l in range(LEVELS):
        bit = (ti >> l) & 1
        start = (ti >> l) << l
        end = start + (1 << l) - 1
        mats.append(np.where(bit == 1, (ii >= start) & (ii <= ti), (ii > ti) & (ii <= end)))
        sel.append(np.broadcast_to(bit, (CHUNK, LANES)))
    mats.append(np.ones((8, CHUNK), bool))
    cstack = np.concatenate(mats, 0).astype(np.float32)
    return (jnp.asarray(cstack, BF16), jnp.asarray(np.stack(sel).astype(np.float32)),
            jnp.asarray(_level_masks(rev)), jnp.asarray(np.eye(CHUNK, dtype=np.float32)))


def _rwkv_consts(rev):
    tau = _tau(rev)
    ti, ii = tau[:, None], tau[None, :]
    cum = np.concatenate([ii <= ti, ii < ti, ii > ti, np.ones((8, CHUNK), bool)], 0).astype(np.float32)
    tri = np.stack([ii < ti, ii <= ti]).astype(np.float32)
    return (jnp.asarray(cum, BF16), jnp.asarray(tri), jnp.asarray(_level_masks(rev)),
            jnp.asarray(np.eye(CHUNK, dtype=np.float32)))


def _norm_proj_kernel(x_ref, g_ref, sc_ref, sh_ref, w_ref, *o_refs, splits):
    x = x_ref[...]
    h = x * lax.rsqrt(jnp.mean(x * x, axis=-1, keepdims=True) + EPS) * g_ref[...]
    h = (h * (1.0 + sc_ref[0]) + sh_ref[0]).astype(BF16)
    off = 0
    for o_ref, n in zip(o_refs, splits):
        o_ref[...] = _dot(h, w_ref[:, off:off + n])
        off += n


def norm_proj(x, g, sc, sh, w, splits, tm=256):
    b, t, d = x.shape
    tm = min(tm, t)
    steps = t // tm
    n = w.shape[1]
    outs = pl.pallas_call(
        functools.partial(_norm_proj_kernel, splits=tuple(splits)),
        grid=(b * steps,),
        in_specs=[pl.BlockSpec((tm, d), lambda i: (i, 0)),
                  _full((1, d)),
                  pl.BlockSpec((1, 1, d), lambda i: (i // steps, 0, 0)),
                  pl.BlockSpec((1, 1, d), lambda i: (i // steps, 0, 0)),
                  pl.BlockSpec((d, n), lambda i: (0, 0), pipeline_mode=pl.Buffered(1))],
        out_specs=[pl.BlockSpec((tm, s), lambda i: (i, 0)) for s in splits],
        out_shape=[jax.ShapeDtypeStruct((b * t, s), F32) for s in splits],
        compiler_params=_params(("parallel",)),
    )(x.reshape(b * t, d), g.reshape(1, d), sc.reshape(b, 1, d), sh.reshape(b, 1, d), w.astype(BF16))
    return [o.reshape(b, t, s) for o, s in zip(outs, splits)]


def _gls_chunk(q, k, g, vs, lane_masks, sts, cstack, sel, lmask, eye):
    ell = CHUNK
    e = jnp.exp(mmc(cstack, g))
    qb = q * e[0:ell]
    ke = k * e[ell:2 * ell]
    dec = e[8 * ell:8 * ell + 1]
    xs = [e[(2 + l) * ell:(3 + l) * ell] * jnp.where(sel[l] > 0.0, q, k) for l in range(LEVELS)]
    outs, new_sts = [], []
    for v, lm, st in zip(vs, lane_masks, sts):
        def msk(z):
            return z if lm is None else z * lm
        att = mm1(msk(q), k, NT) * eye
        for l in range(LEVELS):
            att = att + mm1(msk(xs[l]), xs[l], NT) * lmask[l]
        o = mm1(att, v) + mm1(msk(qb), st, NT)
        st_new = st * dec + mm1(v, msk(ke), TN)
        if lm is not None:
            st_new = st_new * lm
        outs.append(o)
        new_sts.append(st_new)
    return outs, new_sts


def _gls_kernel(*refs, mode, rev, nsub, nchunk):
    if mode == "hgrn":
        (q_ref, v_ref, f_ref, lb_ref, s0_ref, cst_ref, sel_ref, lm_ref, eye_ref,
         o_ref, sfin_ref, st_ref) = refs
    else:
        (q_ref, k_ref, v_ref, gd_ref, w2_ref, gb_ref, hm_ref, s0_ref, cst_ref, sel_ref, lm_ref, eye_ref,
         o_ref, sfin_ref, st_ref) = refs
    t = pl.program_id(1)
    ngroups = q_ref.shape[-1] // LANES

    @pl.when(t == 0)
    def _():
        st_ref[...] = s0_ref[0]

    cstack = cst_ref[...]
    sel = [sel_ref[l] for l in range(LEVELS)]
    lmask = [lm_ref[l] for l in range(LEVELS)]
    eye = eye_ref[...]

    def body(ci, carry):
        c = (nchunk - 1 - ci) if rev else ci
        r0 = pl.multiple_of(c * CHUNK, CHUNK)
        rows = pl.ds(r0, CHUNK)
        if mode == "hgrn":
            qx = q_ref[0, rows, :]
            q_all = _silu(qx) * (HG_DK ** -0.5)
            lb = lb_ref[...]
            f = lb + (1.0 - lb) * jax.nn.sigmoid(f_ref[0, rows, :])
            k_all = 1.0 - f
            g_all = jnp.log(f)
        else:
            q_all = q_ref[0, rows, :] * (GLA_DK ** -0.5)
            k_all = k_ref[0, rows, :]
            z = mm1(gd_ref[0, rows, :], w2_ref[...]) + gb_ref[...]
            g_all = (jnp.minimum(z, 0.0) - jnp.log1p(jnp.exp(-jnp.abs(z)))) * (1.0 / GLA_TAU)
        v_all = v_ref[0, rows, :]
        for gi in range(ngroups):
            ls = slice(gi * LANES, (gi + 1) * LANES)
            hs = [gi * nsub + j for j in range(nsub)]
            vs = [v_all[:, h * LANES:(h + 1) * LANES] for h in hs]
            lms = [None] * nsub if nsub == 1 else [hm_ref[j:j + 1, :] for j in range(nsub)]
            sts = [st_ref[h] for h in hs]
            outs, new_sts = _gls_chunk(q_all[:, ls], k_all[:, ls], g_all[:, ls], vs, lms, sts,
                                       cstack, sel, lmask, eye)
            for h, o, s in zip(hs, outs, new_sts):
                o_ref[0, rows, h * LANES:(h + 1) * LANES] = o
                st_ref[h] = s
        return carry

    lax.fori_loop(0, nchunk, body, 0)

    @pl.when(t == pl.num_programs(1) - 1)
    def _():
        sfin_ref[0] = st_ref[...]


def gated_scan(mode, rev, arrays, s0, tb=512):
    consts = _gls_consts(rev)
    if mode == "hgrn":
        p_hg, lb, d = arrays
        b, t, _ = p_hg.shape
        nh, nsub, width = HG_HEADS, 1, HG_WIDTH
    else:
        q, k, v, gd, w2pad, gb = arrays
        b, t, _ = q.shape
        nh, nsub, width = GLA_HEADS, 2, GLA_WIDTH
    tb = min(tb, t)
    nt = t // tb
    nchunk = tb // CHUNK

    def tmap(i):
        return (nt - 1 - i) if rev else i

    def col(c, w):
        return pl.BlockSpec((1, tb, w), lambda bi, ti: (bi, tmap(ti), c))

    st_spec = pl.BlockSpec((1, nh, s0.shape[2], LANES), lambda bi, ti: (bi, 0, 0, 0))
    cspecs = [_full(c.shape) for c in consts]
    if mode == "hgrn":
        ins = [p_hg, p_hg, p_hg, lb, s0, *consts]
        in_specs = [col(0, width), col(1, width), col(2 + d, width), _full((1, width)), st_spec, *cspecs]
    else:
        hm = np.zeros((2, LANES), np.float32)
        hm[0, :GLA_DK] = 1.0
        hm[1, GLA_DK:] = 1.0
        ins = [q, k, v, gd, w2pad, gb, jnp.asarray(hm), s0, *consts]
        in_specs = [col(0, GLA_QK), col(0, GLA_QK), col(0, width), col(0, 2 * GLA_GATE_LORA),
                    _full(w2pad.shape), _full(gb.shape), _full((2, LANES)), st_spec, *cspecs]
    o, sfin = pl.pallas_call(
        functools.partial(_gls_kernel, mode=mode, rev=rev, nsub=nsub, nchunk=nchunk),
        grid=(b, nt),
        in_specs=in_specs,
        out_specs=[pl.BlockSpec((1, tb, width), lambda bi, ti: (bi, tmap(ti), 0)), st_spec],
        out_shape=[jax.ShapeDtypeStruct((b, t, width), F32), jax.ShapeDtypeStruct(s0.shape, F32)],
        scratch_shapes=[pltpu.VMEM(s0.shape[1:], F32)],
        compiler_params=_params(("parallel", "arbitrary")),
    )(*ins)
    return o, sfin


def _rwkv_prep_kernel(p_ref, pp_ref, pn_ref, mu_ref, w0_ref, w2_ref, a0_ref, a2_ref, g2_ref, kk_ref, ka_ref,
                      rk_ref, ones_ref, r_o, v_o, kk_o, gate_o, bonus_o, lw_o, kd_o, kb_o):
    t = pl.program_id(1)
    nt = pl.num_programs(1)
    p = p_ref[0]
    tb = p.shape[0]
    row = lax.broadcasted_iota(jnp.int32, (tb, 1), 0)
    prev_row = jnp.where(t > 0, pp_ref[0, 7:8, :], 0.0)
    next_row = jnp.where(t < nt - 1, pn_ref[0, 0:1, :], 0.0)
    prev = jnp.where(row == 0, prev_row, pltpu.roll(p, 1, axis=0))
    nxt = jnp.where(row == tb - 1, next_row, pltpu.roll(p, tb - 1, axis=0))
    s = p + mu_ref[0:1, :] * (prev - p) + mu_ref[1:2, :] * (nxt - p)
    w = RW_WIDTH
    r, k, v = s[:, 0:w], s[:, w:2 * w], s[:, 2 * w:3 * w]
    wd = jnp.tanh(s[:, 3 * w:3 * w + LANES])
    ad = s[:, 3 * w + LANES:3 * w + 2 * LANES]
    gd = s[:, 3 * w + 2 * LANES:3 * w + 3 * LANES]
    ones = ones_ref[...]
    kk = k * kk_ref[...]
    nrm = jnp.sqrt(mmcr(kk * kk, ones))
    kk = kk / jnp.maximum(nrm, 1e-12)
    gate = mm1(jax.nn.sigmoid(gd), g2_ref[...])
    ksum = jnp.zeros_like(k)
    for d in range(2):
        zw = w0_ref[d:d + 1, :] + mm1(wd, w2_ref[d])
        w_log = -(jnp.maximum(-zw, 0.0) + jnp.log1p(jnp.exp(-jnp.abs(zw)))) - 0.5
        lw = -jnp.exp(w_log)
        a = jax.nn.sigmoid(a0_ref[d:d + 1, :] + mm1(ad, a2_ref[d]))
        kd = k * (1.0 + (a - 1.0) * ka_ref[...])
        kb = kk * a
        ksum = ksum + kd
        for h in range(RW_HEADS):
            ls = slice(h * RW_DH, (h + 1) * RW_DH)
            lw_o[d, 0, h] = lw[:, ls]
            kd_o[d, 0, h] = kd[:, ls]
            kb_o[d, 0, h] = kb[:, ls]
    bonus = mmcr(r * ksum * rk_ref[...], ones) * v
    for h in range(RW_HEADS):
        ls = slice(h * RW_DH, (h + 1) * RW_DH)
        r_o[0, h] = r[:, ls]
        v_o[0, h] = v[:, ls]
        kk_o[0, h] = kk[:, ls]
        gate_o[0, h] = gate[:, ls]
        bonus_o[0, h] = bonus[:, ls]


def rwkv_prep(p_rw, mu, w0, w2, a0, a2, g2, k_k, k_a, r_k, tb=256):
    b, t, c = p_rw.shape
    tb = min(tb, t)
    nt = t // tb
    w = RW_WIDTH
    z = jnp.zeros((RW_DECAY_LORA, w), F32)
    w2p = jnp.stack([jnp.concatenate([w2[0], z], 0), jnp.concatenate([z, w2[1]], 0)]).astype(BF16)
    a2p = jnp.stack([jnp.concatenate([a2[0], z], 0), jnp.concatenate([z, a2[1]], 0)]).astype(BF16)
    ones = np.kron(np.eye(RW_HEADS, dtype=np.float32), np.ones((RW_DH, RW_DH), np.float32))
    nb8 = t // 8
    hm = jax.ShapeDtypeStruct((b, RW_HEADS, t, RW_DH), F32)
    hm2 = jax.ShapeDtypeStruct((2, b, RW_HEADS, t, RW_DH), F32)
    hm_spec = pl.BlockSpec((1, RW_HEADS, tb, RW_DH), lambda bi, ti: (bi, 0, ti, 0))
    hm2_spec = pl.BlockSpec((2, 1, RW_HEADS, tb, RW_DH), lambda bi, ti: (0, bi, 0, ti, 0))
    return pl.pallas_call(
        _rwkv_prep_kernel,
        grid=(b, nt),
        in_specs=[pl.BlockSpec((1, tb, c), lambda bi, ti: (bi, ti, 0)),
                  pl.BlockSpec((1, 8, c), lambda bi, ti: (bi, jnp.maximum(ti * (tb // 8) - 1, 0), 0)),
                  pl.BlockSpec((1, 8, c), lambda bi, ti: (bi, jnp.minimum((ti + 1) * (tb // 8), nb8 - 1), 0)),
                  _full((2, c)), _full((2, w)), _full((2, LANES, w)), _full((2, w)), _full((2, LANES, w)),
                  _full((RW_GATE_LORA, w)), _full((1, w)), _full((1, w)), _full((1, w)), _full((w, w))],
        out_specs=[hm_spec] * 5 + [hm2_spec] * 3,
        out_shape=[hm] * 5 + [hm2] * 3,
        compiler_params=_params(("parallel", "parallel")),
    )(p_rw, p_rw, p_rw, mu, w0, w2p, a0, a2p, g2.astype(BF16), k_k.reshape(1, w), k_a.reshape(1, w),
      r_k.reshape(1, w), jnp.asarray(ones, BF16))


def _rwkv_chunk(r, k, v, a, b, lw, cum, tri, lmask, eye):
    ell = CHUNK
    c = mmc(cum, lw)
    e_inc = jnp.exp(c[0:ell])
    e_neg = jnp.exp(-c[0:ell])
    e_exc = jnp.exp(c[ell:2 * ell])
    e_rem = jnp.exp(c[2 * ell:3 * ell])
    e_tot = jnp.exp(c[3 * ell:3 * ell + 1])
    rt, at = r * e_inc, a * e_exc
    kt, bt = k * e_neg, b * e_neg
    kh, bh = k * e_rem, b * e_rem
    stril, tril = tri[0], tri[1]
    n = mm3(at, bt, NT) * stril
    m_ak = mm3(at, kt, NT) * stril
    a_rb = mm3(rt, bt, NT) * tril
    a_rk = mm3(rt, kt, NT) * tril
    ti = eye
    for l in range(LEVELS):
        ti = ti + mm3(ti, mm3(n * lmask[l], ti))
    ahat = mm3(ti, at)
    uhat = mm3(ti, mm3(m_ak, v))
    rhat = rt + mm3(a_rb, ahat)
    ohat = mm3(a_rb, uhat) + mm3(a_rk, v)
    g = eye * e_tot + mm3(bh, ahat, TN)
    h = mm3(bh, uhat, TN) + mm3(kh, v, TN)
    return rhat, ohat, g, h


def _rwkv_kernel(r_ref, v_ref, a_ref, lw_ref, k_ref, b_ref, s0_ref, cum_ref, tri_ref, lm_ref, eye_ref,
                 o_ref, sfin_ref, st_ref, *, rev, nchunk, hb):
    t = pl.program_id(2)

    @pl.when(t == 0)
    def _():
        st_ref[...] = s0_ref[0]

    cum = cum_ref[...]
    tri = [tri_ref[0], tri_ref[1]]
    lmask = [lm_ref[l] for l in range(LEVELS)]
    eye = eye_ref[...]

    def body(ci, carry):
        c = (nchunk - 1 - ci) if rev else ci
        rows = pl.ds(pl.multiple_of(c * CHUNK, CHUNK), CHUNK)
        for h in range(hb):
            rhat, ohat, g, hh = _rwkv_chunk(r_ref[0, h, rows, :], k_ref[0, 0, h, rows, :], v_ref[0, h, rows, :],
                                            -a_ref[0, h, rows, :], b_ref[0, 0, h, rows, :],
                                            lw_ref[0, 0, h, rows, :], cum, tri, lmask, eye)
            st = st_ref[h]
            o_ref[0, h, rows, :] = mm3(rhat, st) + ohat
            st_ref[h] = mm3(g, st) + hh
        return carry

    lax.fori_loop(0, nchunk, body, 0)

    @pl.when(t == pl.num_programs(2) - 1)
    def _():
        sfin_ref[0] = st_ref[...]


def rwkv_scan(rev, d, r, v, kk, lw, kd, kb, s0, tb=512, hb=2):
    b, nh, t, dh = r.shape
    tb = min(tb, t)
    nt = t // tb
    nchunk = tb // CHUNK
    consts = _rwkv_consts(rev)

    def tmap(i):
        return (nt - 1 - i) if rev else i

    spec = pl.BlockSpec((1, hb, tb, dh), lambda bi, hi, ti: (bi, hi, tmap(ti), 0))
    spec2 = pl.BlockSpec((1, 1, hb, tb, dh), lambda bi, hi, ti: (d, bi, hi, tmap(ti), 0))
    st_spec = pl.BlockSpec((1, hb, dh, dh), lambda bi, hi, ti: (bi, hi, 0, 0))
    return pl.pallas_call(
        functools.partial(_rwkv_kernel, rev=rev, nchunk=nchunk, hb=hb),
        grid=(b, nh // hb, nt),
        in_specs=[spec, spec, spec, spec2, spec2, spec2, st_spec] + [_full(c.shape) for c in consts],
        out_specs=[spec, st_spec],
        out_shape=[jax.ShapeDtypeStruct(r.shape, F32), jax.ShapeDtypeStruct(s0.shape, F32)],
        scratch_shapes=[pltpu.VMEM((hb, dh, dh), F32)],
        compiler_params=_params(("parallel", "parallel", "arbitrary")),
    )(r, v, kk, lw, kd, kb, s0, *consts)


def _even_out_kernel(x_ref, g1_ref, of_ref, ob_ref, gh_ref, ng_ref, rf_ref, rb_ref, bonus_ref, gate_ref,
                     lng_ref, lnb_ref, w_ref, o_ref):
    z = of_ref[0] + ob_ref[0]
    gh = gh_ref[0]
    acc = None
    for h in range(HG_HEADS):
        ls = slice(h * HG_DV, (h + 1) * HG_DV)
        zh = z[:, ls]
        yh = zh * lax.rsqrt(jnp.mean(zh * zh, axis=-1, keepdims=True) + EPS) * ng_ref[...] * _silu(gh[:, ls])
        part = mm1(yh, w_ref[ls, :])
        acc = part if acc is None else acc + part
    for h in range(RW_HEADS):
        o = rf_ref[0, h] + rb_ref[0, h]
        mean = jnp.mean(o, axis=-1, keepdims=True)
        var = jnp.mean(jnp.square(o - mean), axis=-1, keepdims=True)
        o = (o - mean) * lax.rsqrt(var + RW_GN_EPS) * lng_ref[h] + lnb_ref[h]
        o = (o + bonus_ref[0, h]) * gate_ref[0, h]
        acc = acc + mm1(o, w_ref[HG_WIDTH + h * RW_DH:HG_WIDTH + (h + 1) * RW_DH, :])
    o_ref[0] = x_ref[0] + g1_ref[0] * acc


def even_out(x, g1, o_f, o_b, p_hg, hg_g, rw_f, rw_b, bonus, gate, ln_g, ln_b, w_out, tm=256):
    b, t, d = x.shape
    tm = min(tm, t)
    tok = lambda w: pl.BlockSpec((1, tm, w), lambda bi, ti: (bi, ti, 0))
    hm = pl.BlockSpec((1, RW_HEADS, tm, RW_DH), lambda bi, ti: (bi, 0, ti, 0))
    return pl.pallas_call(
        _even_out_kernel,
        grid=(b, t // tm),
        in_specs=[tok(d), pl.BlockSpec((1, 1, d), lambda bi, ti: (bi, 0, 0)), tok(HG_WIDTH), tok(HG_WIDTH),
                  pl.BlockSpec((1, tm, HG_WIDTH), lambda bi, ti: (bi, ti, 4)), _full((1, HG_DV)),
                  hm, hm, hm, hm, _full((RW_HEADS, 1, RW_DH)), _full((RW_HEADS, 1, RW_DH)),
                  pl.BlockSpec(w_out.shape, lambda bi, ti: (0, 0), pipeline_mode=pl.Buffered(1))],
        out_specs=tok(d),
        out_shape=jax.ShapeDtypeStruct(x.shape, F32),
        compiler_params=_params(("parallel", "parallel")),
    )(x, g1.reshape(b, 1, d), o_f, o_b, p_hg, hg_g.reshape(1, HG_DV), rw_f, rw_b, bonus, gate,
      ln_g.reshape(RW_HEADS, 1, RW_DH), ln_b.reshape(RW_HEADS, 1, RW_DH), w_out.astype(BF16))


def _cpow_table(zr, zi, n):
    def step(c, _):
        cr, ci = c
        return (cr * zr - ci * zi, cr * zi + ci * zr), (cr, ci)
    (_, _), (pr, pi) = lax.scan(step, (jnp.ones_like(zr), jnp.zeros_like(zr)), None, length=n)
    return pr, pi


def _s5_params(a_re, a_im, log_dt, b_re, b_im, c_re, c_im, rev, nsteps):
    ell = S5_CHUNK
    dt = jnp.exp(log_dt)[:, None]
    mag = jnp.exp(a_re * dt)
    lr, li = mag * jnp.cos(a_im * dt), mag * jnp.sin(a_im * dt)
    den = a_re * a_re + a_im * a_im
    fr = ((lr - 1.0) * a_re + li * a_im) / den
    fi = (li * a_re - (lr - 1.0) * a_im) / den
    bbr = fr[..., None] * b_re - fi[..., None] * b_im
    bbi = fr[..., None] * b_im + fi[..., None] * b_re
    pr, pi = _cpow_table(lr, li, ell + 1)
    cr_p = c_re[None] * pr[:, :, None, :] - c_im[None] * pi[:, :, None, :]
    ci_p = c_re[None] * pi[:, :, None, :] + c_im[None] * pr[:, :, None, :]
    kern = jnp.einsum('tgcn,gni->tgci', cr_p[:ell], bbr) - jnp.einsum('tgcn,gni->tgci', ci_p[:ell], bbi)
    s_idx = np.arange(ell)[:, None]
    t_idx = np.arange(ell)[None, :]
    lag = (s_idx - t_idx) if rev else (t_idx - s_idx)
    valid = jnp.asarray(lag >= 0)
    kt = kern[np.clip(lag, 0, ell - 1)]
    kt = jnp.where(valid[:, :, None, None, None], kt, 0.0)
    toep = jnp.transpose(kt, (2, 0, 4, 1, 3)).reshape(S5_GROUPS, S5_FEAT, S5_FEAT)
    e_idx = (np.arange(ell) if rev else (ell - 1 - np.arange(ell)))
    pwr, pwi = pr[e_idx], pi[e_idx]
    p_re = pwr[..., None] * bbr[None] - pwi[..., None] * bbi[None]
    p_im = pwr[..., None] * bbi[None] + pwi[..., None] * bbr[None]
    p_re = jnp.transpose(p_re, (1, 0, 3, 2)).reshape(S5_GROUPS, S5_FEAT, S5_STATE)
    p_im = jnp.transpose(p_im, (1, 0, 3, 2)).reshape(S5_GROUPS, S5_FEAT, S5_STATE)
    q_idx = ((ell - np.arange(ell)) if rev else (np.arange(ell) + 1))
    q_re = jnp.transpose(cr_p[q_idx], (1, 3, 0, 2)).reshape(S5_GROUPS, S5_STATE, S5_FEAT)
    q_im = -jnp.transpose(ci_p[q_idx], (1, 3, 0, 2)).reshape(S5_GROUPS, S5_STATE, S5_FEAT)
    zr, zi = [pr[ell]], [pi[ell]]
    for _ in range(nsteps - 1):
        zr, zi = zr + [zr[-1] * zr[-1] - zi[-1] * zi[-1]], zi + [2.0 * zr[-1] * zi[-1]]
    zr = jnp.stack(zr)[:, :, None, :]
    zi = jnp.stack(zi)[:, :, None, :]
    return toep, p_re, p_im, q_re, q_im, zr, zi


def _s5_kernel(u_ref, toep_ref, pre_ref, pim_ref, qre_ref, qim_ref, zr_ref, zi_ref, h0r_ref, h0i_ref,
               y_ref, hfr_ref, hfi_ref, cr_ref, ci_ref, *, rev, nsteps):
    t = pl.program_id(2)

    @pl.when(t == 0)
    def _():
        cr_ref[...] = h0r_ref[0, 0]
        ci_ref[...] = h0i_ref[0, 0]

    u = u_ref[0]
    jb = u.shape[0]
    row = lax.broadcasted_iota(jnp.int32, (jb, 1), 0)
    edge = (jb - 1) if rev else 0
    xr = mm1(u, pre_ref[0])
    xi = mm1(u, pim_ref[0])
    car_r, car_i = cr_ref[...], ci_ref[...]
    z1r, z1i = zr_ref[0, 0], zi_ref[0, 0]
    at_edge = row == edge
    hr = xr + jnp.where(at_edge, z1r * car_r - z1i * car_i, 0.0)
    hi = xi + jnp.where(at_edge, z1r * car_i + z1i * car_r, 0.0)
    for s in range(nsteps):
        sh = 1 << s
        if sh >= jb:
            break
        zr, zi = zr_ref[s, 0], zi_ref[s, 0]
        if rev:
            pr_ = jnp.where(row < jb - sh, pltpu.roll(hr, jb - sh, axis=0), 0.0)
            pi_ = jnp.where(row < jb - sh, pltpu.roll(hi, jb - sh, axis=0), 0.0)
        else:
            pr_ = jnp.where(row >= sh, pltpu.roll(hr, sh, axis=0), 0.0)
            pi_ = jnp.where(row >= sh, pltpu.roll(hi, sh, axis=0), 0.0)
        hr, hi = hr + (zr * pr_ - zi * pi_), hi + (zr * pi_ + zi * pr_)
    if rev:
        hpr = jnp.where(at_edge, car_r, pltpu.roll(hr, jb - 1, axis=0))
        hpi = jnp.where(at_edge, car_i, pltpu.roll(hi, jb - 1, axis=0))
        cr_ref[...] = hr[0:1]
        ci_ref[...] = hi[0:1]
    else:
        hpr = jnp.where(at_edge, car_r, pltpu.roll(hr, 1, axis=0))
        hpi = jnp.where(at_edge, car_i, pltpu.roll(hi, 1, axis=0))
        cr_ref[...] = hr[jb - 1:jb]
        ci_ref[...] = hi[jb - 1:jb]
    y_ref[0] = mm1(u, toep_ref[0]) + mm1(hpr, qre_ref[0]) + mm1(hpi, qim_ref[0])

    @pl.when(t == pl.num_programs(2) - 1)
    def _():
        hfr_ref[0, 0] = cr_ref[...]
        hfi_ref[0, 0] = ci_ref[...]


def s5_dir(rev, ug, prm, h0r, h0i, nb, jb=128):
    g, rows, feat = ug.shape
    j = rows // nb
    jb = min(jb, j)
    nj = j // jb
    nsteps = max(int(math.log2(jb)), 1)
    toep, p_re, p_im, q_re, q_im, zr, zi = _s5_params(*prm, rev=rev, nsteps=nsteps)

    def rmap(bi, ti):
        return bi * nj + ((nj - 1 - ti) if rev else ti)

    gspec = lambda s: pl.BlockSpec((1,) + s, lambda gi, bi, ti: (gi, 0, 0))
    hspec = pl.BlockSpec((1, 1, 1, S5_STATE), lambda gi, bi, ti: (gi, bi, 0, 0))
    zspec = pl.BlockSpec((nsteps, 1, 1, S5_STATE), lambda gi, bi, ti: (0, gi, 0, 0))
    y, hfr, hfi = pl.pallas_call(
        functools.partial(_s5_kernel, rev=rev, nsteps=nsteps),
        grid=(g, nb, nj),
        in_specs=[pl.BlockSpec((1, jb, feat), lambda gi, bi, ti: (gi, rmap(bi, ti), 0)),
                  gspec((feat, feat)), gspec((feat, S5_STATE)), gspec((feat, S5_STATE)),
                  gspec((S5_STATE, feat)), gspec((S5_STATE, feat)), zspec, zspec, hspec, hspec],
        out_specs=[pl.BlockSpec((1, jb, feat), lambda gi, bi, ti: (gi, rmap(bi, ti), 0)), hspec, hspec],
        out_shape=[jax.ShapeDtypeStruct(ug.shape, F32), jax.ShapeDtypeStruct(h0r.shape, F32),
                   jax.ShapeDtypeStruct(h0i.shape, F32)],
        scratch_shapes=[pltpu.VMEM((1, S5_STATE), F32), pltpu.VMEM((1, S5_STATE), F32)],
        compiler_params=_params(("parallel", "parallel", "arbitrary")),
    )(ug, toep.astype(BF16), p_re.astype(BF16), p_im.astype(BF16), q_re.astype(BF16), q_im.astype(BF16),
      zr, zi, h0r, h0i)
    return y, hfr, hfi


def _fold_chunks(u):
    b, t, _ = u.shape
    z = u.reshape(b, t // S5_CHUNK, S5_CHUNK, S5_GROUPS, S5_GROUP)
    return jnp.transpose(z, (3, 0, 1, 2, 4)).reshape(S5_GROUPS, b * (t // S5_CHUNK), S5_FEAT)


def _unfold_chunks(y, b):
    g, rows, _ = y.shape
    j = rows // b
    z = y.reshape(g, b, j, S5_CHUNK, S5_GROUP)
    return jnp.transpose(z, (1, 2, 3, 0, 4)).reshape(b, j * S5_CHUNK, S5_WIDTH)


def _gelu_tanh(x):
    return 0.5 * x * (1.0 + jnp.tanh(math.sqrt(2.0 / math.pi) * (x + 0.044715 * (x * x * x))))


def _odd_out_kernel(x_ref, g1_ref, of_ref, ob_ref, gg_ref, ng_ref, yf_ref, yb_ref, u_ref, dsk_ref,
                    wglu_ref, bglu_ref, w_ref, o_ref):
    z = of_ref[0] + ob_ref[0]
    gg = gg_ref[0]
    acc = None
    for h in range(GLA_HEADS):
        ls = slice(h * GLA_DV, (h + 1) * GLA_DV)
        zh = z[:, ls]
        yh = zh * lax.rsqrt(jnp.mean(zh * zh, axis=-1, keepdims=True) + EPS) * ng_ref[...] * _silu(gg[:, ls])
        part = mm1(yh, w_ref[ls, :])
        acc = part if acc is None else acc + part
    y = (yf_ref[0] + yb_ref[0]) + dsk_ref[...] * u_ref[0]
    zz = _gelu_tanh(y)
    o_s5 = zz * jax.nn.sigmoid(mm1(zz, wglu_ref[...]) + bglu_ref[...])
    acc = acc + mm1(o_s5, w_ref[GLA_WIDTH:, :])
    o_ref[0] = x_ref[0] + g1_ref[0] * acc


def odd_out(x, g1, o_f, o_b, g_gla, gla_g, y_f, y_b, u, d_skip, w_glu, b_glu, w_out, tm=256):
    b, t, d = x.shape
    tm = min(tm, t)
    tok = lambda w: pl.BlockSpec((1, tm, w), lambda bi, ti: (bi, ti, 0))
    const = lambda a: pl.BlockSpec(a.shape, lambda bi, ti: (0, 0), pipeline_mode=pl.Buffered(1))
    return pl.pallas_call(
        _odd_out_kernel,
        grid=(b, t // tm),
        in_specs=[tok(d), pl.BlockSpec((1, 1, d), lambda bi, ti: (bi, 0, 0)), tok(GLA_WIDTH), tok(GLA_WIDTH),
                  tok(GLA_WIDTH), _full((1, GLA_DV)), tok(S5_WIDTH), tok(S5_WIDTH), tok(S5_WIDTH),
                  _full((1, S5_WIDTH)), const(w_glu), _full((1, S5_WIDTH)), const(w_out)],
        out_specs=tok(d),
        out_shape=jax.ShapeDtypeStruct(x.shape, F32),
        compiler_params=_params(("parallel", "parallel")),
    )(x, g1.reshape(b, 1, d), o_f, o_b, g_gla, gla_g.reshape(1, GLA_DV), y_f, y_b, u,
      d_skip.reshape(1, S5_WIDTH), w_glu.astype(BF16), b_glu.reshape(1, S5_WIDTH), w_out.astype(BF16))


def _ffn_kernel(x_ref, xp_ref, xn_ref, ng_ref, sc_ref, sh_ref, g2_ref, wup_ref, cw_ref, cb_ref, wdn_ref,
                fg_ref, o_ref, *, cols, fw, final):
    r = pl.program_id(1)
    nr = pl.num_programs(1)
    x = x_ref[0]
    n = x.shape[0]

    def modnorm(z):
        h = z * lax.rsqrt(jnp.mean(z * z, axis=-1, keepdims=True) + EPS) * ng_ref[...]
        return (h * (1.0 + sc_ref[0]) + sh_ref[0]).astype(BF16)

    h_main = modnorm(x)
    h_all = jnp.concatenate([modnorm(xp_ref[0]), h_main, modnorm(xn_ref[0])], axis=0)
    na = n + 2 * cols
    pos = lax.broadcasted_iota(jnp.int32, (na, 1), 0)
    colid = jnp.bitwise_and(pos, cols - 1)
    lo = jnp.where(r > 0, 0, cols)
    hi = jnp.where(r < nr - 1, na, cols + n)
    valid = (pos >= lo) & (pos < hi)
    acc = None
    for f0 in range(0, D_FF, fw):
        gate = _dot(h_all, wup_ref[:, f0:f0 + fw])
        gate = jnp.where(valid, gate, 0.0)
        val = _dot(h_main, wup_ref[:, D_FF + f0:D_FF + f0 + fw])
        g_l = jnp.where(colid == 0, 0.0, pltpu.roll(gate, 1, axis=0))
        g_r = jnp.where(colid == cols - 1, 0.0, pltpu.roll(gate, na - 1, axis=0))
        conv = cb_ref[:, f0:f0 + fw]
        for di in range(3):
            rs = slice(di * cols, di * cols + n)
            conv = conv + (g_l[rs] * cw_ref[3 * di + 0:3 * di + 1, f0:f0 + fw]
                           + gate[rs] * cw_ref[3 * di + 1:3 * di + 2, f0:f0 + fw]
                           + g_r[rs] * cw_ref[3 * di + 2:3 * di + 3, f0:f0 + fw])
        act = (_silu(conv) * val).astype(BF16)
        part = _dot(act, wdn_ref[f0:f0 + fw, :])
        acc = part if acc is None else acc + part
    y = x + g2_ref[0] * acc
    if final:
        y = y * lax.rsqrt(jnp.mean(y * y, axis=-1, keepdims=True) + EPS) * fg_ref[...]
    o_ref[0] = y


def conv_ffn(x, ng, sc, sh, g2, w_up, conv_w, conv_b, w_down, final_g, rows, cols, final, rt=8, fw=256):
    b, t, d = x.shape
    rt = min(rt, rows)
    nr = rows // rt
    n = rt * cols
    vec = lambda a: pl.BlockSpec((1, 1, d), lambda bi, ri: (bi, 0, 0))
    const = lambda a: pl.BlockSpec(a.shape, lambda bi, ri: (0, 0), pipeline_mode=pl.Buffered(1))
    wup = w_up.astype(BF16)
    wdn = w_down.astype(BF16)
    cw = conv_w.reshape(9, D_FF)
    cb = conv_b.reshape(1, D_FF)
    return pl.pallas_call(
        functools.partial(_ffn_kernel, cols=cols, fw=fw, final=final),
        grid=(b, nr),
        in_specs=[pl.BlockSpec((1, n, d), lambda bi, ri: (bi, ri, 0)),
                  pl.BlockSpec((1, cols, d), lambda bi, ri: (bi, jnp.maximum(ri * rt - 1, 0), 0)),
                  pl.BlockSpec((1, cols, d), lambda bi, ri: (bi, jnp.minimum((ri + 1) * rt, rows - 1), 0)),
                  _full((1, d)), vec(sc), vec(sh), vec(g2), const(wup), _full((9, D_FF)), _full((1, D_FF)),
                  const(wdn), _full((1, d))],
        out_specs=pl.BlockSpec((1, n, d), lambda bi, ri: (bi, ri, 0)),
        out_shape=jax.ShapeDtypeStruct(x.shape, F32),
        compiler_params=_params(("parallel", "parallel")),
    )(x, x, x, ng.reshape(1, d), sc.reshape(b, 1, d), sh.reshape(b, 1, d), g2.reshape(b, 1, d), wup, cw, cb,
      wdn, final_g.reshape(1, d))


def _even_layer(hx_args, ctx_args, prm, with_ctx_out):
    (w_in, w_out, lb, hg_g, mu, w0, w2, a0, a2, g2, k_k, k_a, r_k, ln_g, ln_b) = prm
    hg_cols = 5 * HG_WIDTH
    res = {}
    bsz = hx_args[0].shape[0]
    s_hg = [jnp.zeros((bsz, HG_HEADS, HG_DV, LANES), F32)] * 2
    s_rw = [jnp.zeros((bsz, RW_HEADS, RW_DH, RW_DH), F32)] * 2
    for name, (x, g, sc, sh, g1), want_out in (("ctx", ctx_args, with_ctx_out), ("x", hx_args, True)):
        p_hg, p_rw = norm_proj(x, g, sc, sh, w_in, (hg_cols, RW_COLS))
        r, v, kk, gate, bonus, lw, kd, kb = rwkv_prep(p_rw, mu, w0, w2, a0, a2, g2, k_k, k_a, r_k)
        o_hg, o_rw = [], []
        for d in range(2):
            o, s_hg[d] = gated_scan("hgrn", d == 1, (p_hg, lb[d:d + 1], d), s_hg[d])
            o_hg.append(o)
            o, s_rw[d] = rwkv_scan(d == 1, d, r, v, kk, lw, kd, kb, s_rw[d])
            o_rw.append(o)
        if want_out:
            res[name] = even_out(x, g1, o_hg[0], o_hg[1], p_hg, hg_g, o_rw[0], o_rw[1], bonus, gate,
                                 ln_g, ln_b, w_out)
    return res


def _odd_layer(hx_args, ctx_args, prm, with_ctx_out):
    (w_in, w_out, gla_w2, gla_b, gla_g, a_re, a_im, log_dt, b_re, b_im, c_re, c_im, d_skip, w_glu, b_glu) = prm
    nq = 2 * GLA_QK + 2 * GLA_WIDTH
    w_perm = jnp.concatenate([w_in[:, :nq], w_in[:, nq + 2 * GLA_GATE_LORA:], w_in[:, nq:nq + 2 * GLA_GATE_LORA]], 1)
    z16 = jnp.zeros((GLA_GATE_LORA, GLA_QK), F32)
    w2pad = [jnp.concatenate([gla_w2[0], z16], 0).astype(BF16), jnp.concatenate([z16, gla_w2[1]], 0).astype(BF16)]
    res = {}
    bsz = hx_args[0].shape[0]
    s_gla = [jnp.zeros((bsz, GLA_HEADS, GLA_DV, LANES), F32)] * 2
    zs = jnp.zeros((S5_GROUPS, bsz, 1, S5_STATE), F32)
    s_s5 = [(zs, zs), (zs, zs)]
    for name, (x, g, sc, sh, g1), want_out in (("ctx", ctx_args, with_ctx_out), ("x", hx_args, True)):
        q, k, v, gg, u, gd = norm_proj(x, g, sc, sh, w_perm,
                                       (GLA_QK, GLA_QK, GLA_WIDTH, GLA_WIDTH, S5_WIDTH, 2 * GLA_GATE_LORA))
        ug = _fold_chunks(u)
        o_gla, y_s5 = [], []
        for d in range(2):
            o, s_gla[d] = gated_scan("gla", d == 1, (q, k, v, gd, w2pad[d], gla_b[d].reshape(1, GLA_QK)), s_gla[d])
            o_gla.append(o)
            prm_d = (a_re[d], a_im[d], log_dt[d], b_re[d], b_im[d], c_re[d], c_im[d])
            y, hr, hi = s5_dir(d == 1, ug, prm_d, s_s5[d][0], s_s5[d][1], bsz)
            s_s5[d] = (hr, hi)
            y_s5.append(_unfold_chunks(y, bsz))
        if want_out:
            res[name] = odd_out(x, g1, o_gla[0], o_gla[1], gg, gla_g, y_s5[0], y_s5[1], u, d_skip, w_glu, b_glu,
                                w_out)
    return res


def kernel(x, c, ctx, c_ctx, ada_w, ada_b, norm1_g, norm2_g, final_g,
           ev_w_in, ev_w_out, hg_lb, hg_norm_g, rw_mu, rw_w0, rw_w2, rw_a0, rw_a2, rw_g2,
           rw_k_k, rw_k_a, rw_r_k, rw_ln_g, rw_ln_b,
           od_w_in, od_w_out, gla_w2, gla_b, gla_norm_g, s5_a_re, s5_a_im, s5_log_dt,
           s5_b_re, s5_b_im, s5_c_re, s5_c_im, s5_d, s5_w_glu, s5_b_glu,
           ffn_w_up, ffn_conv_w, ffn_conv_b, ffn_w_down):
    bsz, seq, d = x.shape
    depth = ada_w.shape[0]
    rows = seq // GRID_W
    ctx_len = ctx.shape[1]
    lb_all = jnp.cumsum(jax.nn.softmax(hg_lb.astype(F32), axis=0), axis=0)

    for layer in range(depth):
        last = layer == depth - 1
        j = layer // 2
        mod = jnp.split(jax.nn.silu(c) @ ada_w[layer] + ada_b[layer], 6, axis=-1)
        mod_c = jnp.split(jax.nn.silu(c_ctx) @ ada_w[layer] + ada_b[layer], 6, axis=-1)
        sh1, sc1, g1, sh2, sc2, g2 = mod
        csh1, csc1, cg1, csh2, csc2, cg2 = [jnp.broadcast_to(m[None], (bsz, d)) for m in mod_c]
        hx_args = (x, norm1_g[layer], sc1, sh1, g1)
        ctx_args = (ctx, norm1_g[layer], csc1, csh1, cg1)
        if layer % 2 == 0:
            prm = (ev_w_in[j], ev_w_out[j], lb_all[j], hg_norm_g[j], rw_mu[j], rw_w0[j], rw_w2[j], rw_a0[j],
                   rw_a2[j], rw_g2[j], rw_k_k[j], rw_k_a[j], rw_r_k[j], rw_ln_g[j], rw_ln_b[j])
            res = _even_layer(hx_args, ctx_args, prm, not last)
        else:
            prm = (od_w_in[j], od_w_out[j], gla_w2[j], gla_b[j], gla_norm_g[j], s5_a_re[j], s5_a_im[j],
                   s5_log_dt[j], s5_b_re[j], s5_b_im[j], s5_c_re[j], s5_c_im[j], s5_d[j], s5_w_glu[j], s5_b_glu[j])
            res = _odd_layer(hx_args, ctx_args, prm, not last)
        ffn = (ffn_w_up[layer], ffn_conv_w[layer], ffn_conv_b[layer], ffn_w_down[layer], final_g)
        x = conv_ffn(res["x"], norm2_g[layer], sc2, sh2, g2, *ffn, rows, GRID_W, final=last)
        if not last:
            ctx = conv_ffn(res["ctx"], norm2_g[layer], csc2, csh2, cg2, *ffn, 1, ctx_len, final=False)
    return x
```

```python
import functools
import math

import numpy as np
import jax
import jax.numpy as jnp
from jax import lax
from jax.experimental import pallas as pl
from jax.experimental.pallas import tpu as pltpu

F32 = jnp.float32
BF16 = jnp.bfloat16

EPS = 1e-6
GRID_W = 64
CHUNK = 64
LEVELS = 6

HG_HEADS, HG_DK, HG_DV = 4, 128, 128
HG_WIDTH = HG_HEADS * HG_DV
RW_HEADS, RW_DH = 8, 64
RW_WIDTH = RW_HEADS * RW_DH
RW_DECAY_LORA, RW_AAA_LORA, RW_GATE_LORA = 64, 64, 128
RW_GN_EPS = 64e-5
RW_COLS = 3 * RW_WIDTH + 2 * RW_DECAY_LORA + 2 * RW_AAA_LORA + RW_GATE_LORA
GLA_HEADS, GLA_DK, GLA_DV = 4, 64, 128
GLA_WIDTH = GLA_HEADS * GLA_DV
GLA_QK = GLA_HEADS * GLA_DK
GLA_GATE_LORA = 16
GLA_TAU = 16.0
S5_WIDTH, S5_GROUP, S5_STATE = 512, 16, 64
S5_GROUPS = S5_WIDTH // S5_GROUP
S5_CHUNK = 16
S5_FEAT = S5_CHUNK * S5_GROUP
D_FF = 2816

LANES = 128
VMEM_LIMIT = 56 * 1024 * 1024

NN = ((1,), (0,))
NT = ((1,), (1,))
TN = ((0,), (0,))


def _dot(a, b, dims=NN):
    return lax.dot_general(a, b, (dims, ((), ())), preferred_element_type=F32)


def _split(a):
    hi = a.astype(BF16)
    return hi, (a - hi.astype(F32)).astype(BF16)


def mm1(a, b, dims=NN):
    return _dot(a.astype(BF16), b.astype(BF16), dims)


def mm3(a, b, dims=NN):
    ah, al = _split(a)
    bh, bl = _split(b)
    return _dot(ah, bh, dims) + (_dot(al, bh, dims) + _dot(ah, bl, dims))


def mmc(c, x, dims=NN):
    xh, xl = _split(x)
    return _dot(c, xh, dims) + _dot(c, xl, dims)


def mmcr(x, c, dims=NN):
    xh, xl = _split(x)
    return _dot(xh, c, dims) + _dot(xl, c, dims)


def _silu(x):
    return x * jax.nn.sigmoid(x)


def _params(sem):
    return pltpu.CompilerParams(dimension_semantics=sem, vmem_limit_bytes=VMEM_LIMIT)


def _full(shape):
    nd = len(shape)
    return pl.BlockSpec(shape, lambda *_: (0,) * nd)


def _tau(rev):
    t = np.arange(CHUNK)
    return (CHUNK - 1 - t) if rev else t


def _level_masks(rev):
    tau = _tau(rev)
    ti, si = tau[:, None], tau[None, :]
    out = []
    for l in range(LEVELS):
        m = (((ti >> l) & 1) == 1) & (((si >> l) & 1) == 0) & ((ti >> (l + 1)) == (si >> (l + 1)))
        out.append(m)
    return np.stack(out).astype(np.float32)


def _gls_consts(rev):
    tau = _tau(rev)
    ti, ii = tau[:, None], tau[None, :]
    mats = [ii <= ti, ii > ti]
    sel = []
    for l in range(LEVELS):
        bit = (ti >> l) & 1
        start = (ti >> l) << l
        end = start + (1 << l) - 1
        mats.append(np.where(bit == 1, (ii >= start) & (ii <= ti), (ii > ti) & (ii <= end)))
        sel.append(np.broadcast_to(bit, (CHUNK, LANES)))
    mats.append(np.ones((8, CHUNK), bool))
    cstack = np.concatenate(mats, 0).astype(np.float32)
    return (jnp.asarray(cstack, BF16), jnp.asarray(np.stack(sel).astype(np.float32)),
            jnp.asarray(_level_masks(rev)), jnp.asarray(np.eye(CHUNK, dtype=np.float32)))


def _rwkv_consts(rev):
    tau = _tau(rev)
    ti, ii = tau[:, None], tau[None, :]
    cum = np.concatenate([ii <= ti, ii < ti, ii > ti, np.ones((8, CHUNK), bool)], 0).astype(np.float32)
    tri = np.stack([ii < ti, ii <= ti]).astype(np.float32)
    pair = lambda m: np.concatenate([m, m], axis=-1)
    hmask = np.zeros((2, LANES), np.float32)
    hmask[0, :RW_DH] = 1.0
    hmask[1, RW_DH:] = 1.0
    return (jnp.asarray(cum, BF16), jnp.asarray(pair(tri)), jnp.asarray(pair(_level_masks(rev))),
            jnp.asarray(pair(np.eye(CHUNK, dtype=np.float32))), jnp.asarray(hmask))


def _norm_proj_kernel(x_ref, g_ref, sc_ref, sh_ref, w_ref, *o_refs, splits):
    x = x_ref[...]
    h = x * lax.rsqrt(jnp.mean(x * x, axis=-1, keepdims=True) + EPS) * g_ref[...]
    h = (h * (1.0 + sc_ref[0]) + sh_ref[0]).astype(BF16)
    off = 0
    for o_ref, n in zip(o_refs, splits):
        o_ref[...] = _dot(h, w_ref[:, off:off + n])
        off += n


def norm_proj(x, g, sc, sh, w, splits, tm=256):
    b, t, d = x.shape
    tm = min(tm, t)
    steps = t // tm
    n = w.shape[1]
    outs = pl.pallas_call(
        functools.partial(_norm_proj_kernel, splits=tuple(splits)),
        grid=(b * steps,),
        in_specs=[pl.BlockSpec((tm, d), lambda i: (i, 0)),
                  _full((1, d)),
                  pl.BlockSpec((1, 1, d), lambda i: (i // steps, 0, 0)),
                  pl.BlockSpec((1, 1, d), lambda i: (i // steps, 0, 0)),
                  pl.BlockSpec((d, n), lambda i: (0, 0), pipeline_mode=pl.Buffered(1))],
        out_specs=[pl.BlockSpec((tm, s), lambda i: (i, 0)) for s in splits],
        out_shape=[jax.ShapeDtypeStruct((b * t, s), F32) for s in splits],
        compiler_params=_params(("parallel",)),
    )(x.reshape(b * t, d), g.reshape(1, d), sc.reshape(b, 1, d), sh.reshape(b, 1, d), w.astype(BF16))
    return [o.reshape(b, t, s) for o, s in zip(outs, splits)]


def _gls_chunks(groups, heads, cstack, sel, lmask, eye):
    ell = CHUNK
    c2 = jnp.concatenate([cstack, cstack], axis=1)
    es = [jnp.exp(_dot(c2, jnp.concatenate(_split(g), axis=0))) for _, _, g in groups]
    xs = [[e[(2 + l) * ell:(3 + l) * ell] * jnp.where(sel[l] > 0.0, q, k) for l in range(LEVELS)]
          for (q, k, _), e in zip(groups, es)]

    def msk(z, lm):
        return z if lm is None else z * lm

    att = [mm1(msk(groups[gi][0], lm), groups[gi][1], NT) * eye for gi, _, lm in heads]
    for l in range(LEVELS):
        att = [a + mm1(msk(xs[gi][l], lm), xs[gi][l], NT) * lmask[l] for a, (gi, _, lm) in zip(att, heads)]
    ov = [mm1(a, v) for a, (_, v, _) in zip(att, heads)]
    qb = [msk(groups[gi][0] * es[gi][0:ell], lm) for gi, _, lm in heads]
    kv = [mm1(v, msk(groups[gi][1] * es[gi][ell:2 * ell], lm), TN) for gi, v, lm in heads]
    dec = [e[8 * ell:8 * ell + 1] for e in es]
    return ov, qb, kv, dec


def _gls_kernel(*refs, mode, rev, nsub, nchunk, cu):
    if mode == "hgrn":
        (q_ref, v_ref, f_ref, lb_ref, s0_ref, cst_ref, sel_ref, lm_ref, eye_ref,
         o_ref, sfin_ref, st_ref) = refs
    else:
        (q_ref, k_ref, v_ref, gd_ref, w2_ref, gb_ref, hm_ref, s0_ref, cst_ref, sel_ref, lm_ref, eye_ref,
         o_ref, sfin_ref, st_ref) = refs
    t = pl.program_id(1)
    ngroups = q_ref.shape[-1] // LANES

    @pl.when(t == 0)
    def _():
        st_ref[...] = s0_ref[0]

    cstack = cst_ref[...]
    sel = [sel_ref[l] for l in range(LEVELS)]
    lmask = [lm_ref[l] for l in range(LEVELS)]
    eye = eye_ref[...]

    def body(it, carry):
        nh = ngroups * nsub
        state = [st_ref[h] for h in range(nh)]
        rows, loaded = [], []
        for j in range(cu):
            ci = it * cu + j
            c = (nchunk - 1 - ci) if rev else ci
            rows.append(pl.ds(pl.multiple_of(c * CHUNK, CHUNK), CHUNK))
            if mode == "hgrn":
                loaded.append((q_ref[0, rows[j], :], f_ref[0, rows[j], :], v_ref[0, rows[j], :]))
            else:
                loaded.append((q_ref[0, rows[j], :], k_ref[0, rows[j], :], gd_ref[0, rows[j], :],
                               v_ref[0, rows[j], :]))
        groups, heads = [], []
        for j in range(cu):
            if mode == "hgrn":
                qx, fx, v_all = loaded[j]
                q_all = _silu(qx) * (HG_DK ** -0.5)
                lb = lb_ref[...]
                f = lb + (1.0 - lb) * jax.nn.sigmoid(fx)
                k_all = 1.0 - f
                g_all = jnp.log(f)
            else:
                qx, k_all, gdx, v_all = loaded[j]
                q_all = qx * (GLA_DK ** -0.5)
                z = mm1(gdx, w2_ref[...]) + gb_ref[...]
                g_all = (jnp.minimum(z, 0.0) - jnp.log1p(jnp.exp(-jnp.abs(z)))) * (1.0 / GLA_TAU)
            for gi in range(ngroups):
                ls = slice(gi * LANES, (gi + 1) * LANES)
                groups.append((q_all[:, ls], k_all[:, ls], g_all[:, ls]))
                for i in range(nsub):
                    h = gi * nsub + i
                    heads.append((j * ngroups + gi, v_all[:, h * LANES:(h + 1) * LANES],
                                  None if nsub == 1 else hm_ref[i:i + 1, :]))
        ov, qb, kv, dec = _gls_chunks(groups, heads, cstack, sel, lmask, eye)
        results = []
        for idx, (gidx, _, lm) in enumerate(heads):
            j, h = idx // nh, idx % nh
            results.append((j, h, ov[idx] + mm1(qb[idx], state[h], NT)))
            st_new = state[h] * dec[gidx] + kv[idx]
            state[h] = st_new if lm is None else st_new * lm
        for j, h, o in results:
            o_ref[0, rows[j], h * LANES:(h + 1) * LANES] = o
        for h in range(nh):
            st_ref[h] = state[h]
        return carry

    lax.fori_loop(0, nchunk // cu, body, 0)

    @pl.when(t == pl.num_programs(1) - 1)
    def _():
        sfin_ref[0] = st_ref[...]


def gated_scan(mode, rev, arrays, s0, tb=512, cu=2):
    consts = _gls_consts(rev)
    if mode == "hgrn":
        p_hg, lb, d = arrays
        b, t, _ = p_hg.shape
        nh, nsub, width = HG_HEADS, 1, HG_WIDTH
    else:
        q, k, v, gd, w2pad, gb = arrays
        b, t, _ = q.shape
        nh, nsub, width = GLA_HEADS, 2, GLA_WIDTH
    tb = min(tb, t)
    nt = t // tb
    nchunk = tb // CHUNK

    def tmap(i):
        return (nt - 1 - i) if rev else i

    def col(c, w):
        return pl.BlockSpec((1, tb, w), lambda bi, ti: (bi, tmap(ti), c))

    st_spec = pl.BlockSpec((1, nh, s0.shape[2], LANES), lambda bi, ti: (bi, 0, 0, 0))
    cspecs = [_full(c.shape) for c in consts]
    if mode == "hgrn":
        ins = [p_hg, p_hg, p_hg, lb, s0, *consts]
        in_specs = [col(0, width), col(1, width), col(2 + d, width), _full((1, width)), st_spec, *cspecs]
    else:
        hm = np.zeros((2, LANES), np.float32)
        hm[0, :GLA_DK] = 1.0
        hm[1, GLA_DK:] = 1.0
        ins = [q, k, v, gd, w2pad, gb, jnp.asarray(hm), s0, *consts]
        in_specs = [col(0, GLA_QK), col(0, GLA_QK), col(0, width), col(0, 2 * GLA_GATE_LORA),
                    _full(w2pad.shape), _full(gb.shape), _full((2, LANES)), st_spec, *cspecs]
    o, sfin = pl.pallas_call(
        functools.partial(_gls_kernel, mode=mode, rev=rev, nsub=nsub, nchunk=nchunk, cu=cu),
        grid=(b, nt),
        in_specs=in_specs,
        out_specs=[pl.BlockSpec((1, tb, width), lambda bi, ti: (bi, tmap(ti), 0)), st_spec],
        out_shape=[jax.ShapeDtypeStruct((b, t, width), F32), jax.ShapeDtypeStruct(s0.shape, F32)],
        scratch_shapes=[pltpu.VMEM(s0.shape[1:], F32)],
        compiler_params=_params(("parallel", "arbitrary")),
    )(*ins)
    return o, sfin


def _rwkv_prep_kernel(p_ref, pp_ref, pn_ref, mu_ref, w0_ref, w2_ref, a0_ref, a2_ref, g2_ref, kk_ref, ka_ref,
                      rk_ref, ones_ref, r_o, v_o, kk_o, gate_o, bonus_o, lw_o, kd_o, kb_o):
    t = pl.program_id(1)
    nt = pl.num_programs(1)
    p = p_ref[0]
    tb = p.shape[0]
    row = lax.broadcasted_iota(jnp.int32, (tb, 1), 0)
    prev_row = jnp.where(t > 0, pp_ref[0, 7:8, :], 0.0)
    next_row = jnp.where(t < nt - 1, pn_ref[0, 0:1, :], 0.0)
    prev = jnp.where(row == 0, prev_row, pltpu.roll(p, 1, axis=0))
    nxt = jnp.where(row == tb - 1, next_row, pltpu.roll(p, tb - 1, axis=0))
    s = p + mu_ref[0:1, :] * (prev - p) + mu_ref[1:2, :] * (nxt - p)
    w = RW_WIDTH
    r, k, v = s[:, 0:w], s[:, w:2 * w], s[:, 2 * w:3 * w]
    wd = jnp.tanh(s[:, 3 * w:3 * w + LANES])
    ad = s[:, 3 * w + LANES:3 * w + 2 * LANES]
    gd = s[:, 3 * w + 2 * LANES:3 * w + 3 * LANES]
    ones = ones_ref[...]
    kk = k * kk_ref[...]
    nrm = jnp.sqrt(mmcr(kk * kk, ones))
    kk = kk / jnp.maximum(nrm, 1e-12)
    gate = mm1(jax.nn.sigmoid(gd), g2_ref[...])
    ksum = jnp.zeros_like(k)
    for d in range(2):
        zw = w0_ref[d:d + 1, :] + mm1(wd, w2_ref[d])
        w_log = -(jnp.maximum(-zw, 0.0) + jnp.log1p(jnp.exp(-jnp.abs(zw)))) - 0.5
        lw = -jnp.exp(w_log)
        a = jax.nn.sigmoid(a0_ref[d:d + 1, :] + mm1(ad, a2_ref[d]))
        kd = k * (1.0 + (a - 1.0) * ka_ref[...])
        ksum = ksum + kd
        lw_o[d, 0] = lw
        kd_o[d, 0] = kd
        kb_o[d, 0] = kk * a
    r_o[0] = r
    v_o[0] = v
    kk_o[0] = kk
    gate_o[0] = gate
    bonus_o[0] = mmcr(r * ksum * rk_ref[...], ones) * v


def rwkv_prep(p_rw, mu, w0, w2, a0, a2, g2, k_k, k_a, r_k, tb=256):
    b, t, c = p_rw.shape
    tb = min(tb, t)
    nt = t // tb
    w = RW_WIDTH
    z = jnp.zeros((RW_DECAY_LORA, w), F32)
    w2p = jnp.stack([jnp.concatenate([w2[0], z], 0), jnp.concatenate([z, w2[1]], 0)]).astype(BF16)
    a2p = jnp.stack([jnp.concatenate([a2[0], z], 0), jnp.concatenate([z, a2[1]], 0)]).astype(BF16)
    ones = np.kron(np.eye(RW_HEADS, dtype=np.float32), np.ones((RW_DH, RW_DH), np.float32))
    nb8 = t // 8
    hm = jax.ShapeDtypeStruct((b, t, w), F32)
    hm2 = jax.ShapeDtypeStruct((2, b, t, w), F32)
    hm_spec = pl.BlockSpec((1, tb, w), lambda bi, ti: (bi, ti, 0))
    hm2_spec = pl.BlockSpec((2, 1, tb, w), lambda bi, ti: (0, bi, ti, 0))
    return pl.pallas_call(
        _rwkv_prep_kernel,
        grid=(b, nt),
        in_specs=[pl.BlockSpec((1, tb, c), lambda bi, ti: (bi, ti, 0)),
                  pl.BlockSpec((1, 8, c), lambda bi, ti: (bi, jnp.maximum(ti * (tb // 8) - 1, 0), 0)),
                  pl.BlockSpec((1, 8, c), lambda bi, ti: (bi, jnp.minimum((ti + 1) * (tb // 8), nb8 - 1), 0)),
                  _full((2, c)), _full((2, w)), _full((2, LANES, w)), _full((2, w)), _full((2, LANES, w)),
                  _full((RW_GATE_LORA, w)), _full((1, w)), _full((1, w)), _full((1, w)), _full((w, w))],
        out_specs=[hm_spec] * 5 + [hm2_spec] * 3,
        out_shape=[hm] * 5 + [hm2] * 3,
        compiler_params=_params(("parallel", "parallel")),
    )(p_rw, p_rw, p_rw, mu, w0, w2p, a0, a2p, g2.astype(BF16), k_k.reshape(1, w), k_a.reshape(1, w),
      r_k.reshape(1, w), jnp.asarray(ones, BF16))


def _bd(y, hm):
    return jnp.concatenate([y * hm[0], y * hm[1]], axis=0).astype(BF16)


def _bdiag(z, hm):
    return z[:CHUNK] * hm[0] + z[CHUNK:] * hm[1]


def _rwkv_chunks(ins, cum, tri, lmask, eye, hm):
    ell = CHUNK
    nc = range(len(ins))
    cum2 = jnp.concatenate([cum, cum], axis=1)
    cs = [_dot(cum2, jnp.concatenate(_split(x[5]), axis=0)) for x in ins]
    rt = [ins[i][0] * jnp.exp(cs[i][0:ell]) for i in nc]
    at = [ins[i][3] * jnp.exp(cs[i][ell:2 * ell]) for i in nc]
    e_neg = [jnp.exp(-cs[i][0:ell]) for i in nc]
    e_rem = [jnp.exp(cs[i][2 * ell:3 * ell]) for i in nc]
    e_tot = [jnp.exp(cs[i][3 * ell:3 * ell + 1]) for i in nc]
    ar = [jnp.concatenate([at[i], rt[i]], axis=0).astype(BF16) for i in nc]
    pb = [_dot(ar[i], _bd(ins[i][4] * e_neg[i], hm), NT) for i in nc]
    pk = [_dot(ar[i], _bd(ins[i][1] * e_neg[i], hm), NT) for i in nc]
    n = [pb[i][:ell] * tri[0] for i in nc]
    ti = [eye + n[i] * lmask[0] for i in nc]
    for l in range(1, LEVELS):
        p = [_dot((n[i] * lmask[l]).astype(BF16), _bd(ti[i], hm)) for i in nc]
        ti = [ti[i] + _dot(ti[i].astype(BF16), _bd(p[i], hm)) for i in nc]
    vbd = [_bd(ins[i][2], hm) for i in nc]
    mv = [_dot((pk[i][:ell] * tri[0]).astype(BF16), vbd[i]) for i in nc]
    au = [_dot(ti[i].astype(BF16), jnp.concatenate([_bd(at[i], hm), _bd(mv[i], hm)], axis=1)) for i in nc]
    ro = [_dot((pb[i][ell:] * tri[1]).astype(BF16),
               jnp.concatenate([_bd(au[i][:, :LANES], hm), _bd(au[i][:, LANES:], hm)], axis=1)) for i in nc]
    rk = [_dot((pk[i][ell:] * tri[1]).astype(BF16), vbd[i]) for i in nc]
    gh = [_dot((ins[i][4] * e_rem[i]).astype(BF16), au[i].astype(BF16), TN) for i in nc]
    kv = [_dot((ins[i][1] * e_rem[i]).astype(BF16), ins[i][2].astype(BF16), TN) for i in nc]
    out = []
    for i in nc:
        rhat = rt[i] + ro[i][:, :LANES]
        ohat = ro[i][:, LANES:] + rk[i]
        g = eye * e_tot[i] + _bdiag(gh[i][:, :LANES], hm)
        h = _bdiag(gh[i][:, LANES:], hm) + _bdiag(kv[i], hm)
        out.append((rhat, ohat, g, h))
    return out


def _rwkv_state_mm(x, st, hm):
    s_hi = st.astype(BF16).astype(F32)
    bdh, bdl = _bd(s_hi, hm), _bd(st - s_hi, hm)
    xh, xl = _split(x)
    return _dot(jnp.concatenate([xh, xl], axis=1), jnp.concatenate([bdh, bdh], axis=0)) + _dot(xh, bdl)


def _rwkv_kernel(r_ref, v_ref, a_ref, lw_ref, k_ref, b_ref, s0_ref, cum_ref, tri_ref, lm_ref, eye_ref, hm_ref,
                 o_ref, sfin_ref, st_ref, *, rev, nchunk, cu):
    t = pl.program_id(1)
    npair = r_ref.shape[-1] // LANES

    @pl.when(t == 0)
    def _():
        st_ref[...] = s0_ref[0]

    cum = cum_ref[...]
    tri = [tri_ref[0], tri_ref[1]]
    lmask = [lm_ref[l] for l in range(LEVELS)]
    eye = eye_ref[...]
    hm = [hm_ref[0:1, :], hm_ref[1:2, :]]

    def body(it, carry):
        sts = [st_ref[p] for p in range(npair)]
        rows, ins = [], []
        for j in range(cu):
            ci = it * cu + j
            c = (nchunk - 1 - ci) if rev else ci
            rows.append(pl.ds(pl.multiple_of(c * CHUNK, CHUNK), CHUNK))
            for p in range(npair):
                ls = slice(p * LANES, (p + 1) * LANES)
                ins.append((r_ref[0, rows[j], ls], k_ref[0, 0, rows[j], ls], v_ref[0, rows[j], ls],
                            -a_ref[0, rows[j], ls], b_ref[0, 0, rows[j], ls], lw_ref[0, 0, rows[j], ls]))
        preps = _rwkv_chunks(ins, cum, tri, lmask, eye, hm)
        outs = []
        for j in range(cu):
            for p in range(npair):
                rhat, ohat, g, hh = preps[j * npair + p]
                res = _rwkv_state_mm(jnp.concatenate([rhat, g], axis=0), sts[p], hm)
                outs.append(res[:CHUNK] + ohat)
                sts[p] = res[CHUNK:] + hh
        for j in range(cu):
            for p in range(npair):
                o_ref[0, rows[j], p * LANES:(p + 1) * LANES] = outs[j * npair + p]
        for p in range(npair):
            st_ref[p] = sts[p]
        return carry

    lax.fori_loop(0, nchunk // cu, body, 0)

    @pl.when(t == pl.num_programs(1) - 1)
    def _():
        sfin_ref[0] = st_ref[...]


def rwkv_scan(rev, d, r, v, kk, lw, kd, kb, s0, tb=512, cu=2):
    b, t, w = r.shape
    tb = min(tb, t)
    nt = t // tb
    nchunk = tb // CHUNK
    consts = _rwkv_consts(rev)

    def tmap(i):
        return (nt - 1 - i) if rev else i

    spec = pl.BlockSpec((1, tb, w), lambda bi, ti: (bi, tmap(ti), 0))
    spec2 = pl.BlockSpec((1, 1, tb, w), lambda bi, ti: (d, bi, tmap(ti), 0))
    st_spec = pl.BlockSpec((1,) + s0.shape[1:], lambda bi, ti: (bi, 0, 0, 0))
    return pl.pallas_call(
        functools.partial(_rwkv_kernel, rev=rev, nchunk=nchunk, cu=cu),
        grid=(b, nt),
        in_specs=[spec, spec, spec, spec2, spec2, spec2, st_spec] + [_full(c.shape) for c in consts],
        out_specs=[spec, st_spec],
        out_shape=[jax.ShapeDtypeStruct(r.shape, F32), jax.ShapeDtypeStruct(s0.shape, F32)],
        scratch_shapes=[pltpu.VMEM(s0.shape[1:], F32)],
        compiler_params=_params(("parallel", "arbitrary")),
    )(r, v, kk, lw, kd, kb, s0, *consts)


def _even_out_kernel(x_ref, g1_ref, of_ref, ob_ref, gh_ref, ng_ref, rf_ref, rb_ref, bonus_ref, gate_ref,
                     lng_ref, lnb_ref, ones_ref, w_ref, o_ref):
    z = of_ref[0] + ob_ref[0]
    gh = gh_ref[0]
    acc = None
    for h in range(HG_HEADS):
        ls = slice(h * HG_DV, (h + 1) * HG_DV)
        zh = z[:, ls]
        yh = zh * lax.rsqrt(jnp.mean(zh * zh, axis=-1, keepdims=True) + EPS) * ng_ref[...] * _silu(gh[:, ls])
        part = mm1(yh, w_ref[ls, :])
        acc = part if acc is None else acc + part
    o = rf_ref[0] + rb_ref[0]
    ones = ones_ref[...]
    mean = mmcr(o, ones) * (1.0 / RW_DH)
    cen = o - mean
    var = mmcr(cen * cen, ones) * (1.0 / RW_DH)
    o = cen * lax.rsqrt(var + RW_GN_EPS) * lng_ref[...] + lnb_ref[...]
    o = (o + bonus_ref[0]) * gate_ref[0]
    acc = acc + mm1(o, w_ref[HG_WIDTH:, :])
    o_ref[0] = x_ref[0] + g1_ref[0] * acc


def even_out(x, g1, o_f, o_b, p_hg, hg_g, rw_f, rw_b, bonus, gate, ln_g, ln_b, w_out, tm=256):
    b, t, d = x.shape
    tm = min(tm, t)
    tok = lambda w: pl.BlockSpec((1, tm, w), lambda bi, ti: (bi, ti, 0))
    hm = tok(RW_WIDTH)
    ones = np.kron(np.eye(RW_HEADS, dtype=np.float32), np.ones((RW_DH, RW_DH), np.float32))
    return pl.pallas_call(
        _even_out_kernel,
        grid=(b, t // tm),
        in_specs=[tok(d), pl.BlockSpec((1, 1, d), lambda bi, ti: (bi, 0, 0)), tok(HG_WIDTH), tok(HG_WIDTH),
                  pl.BlockSpec((1, tm, HG_WIDTH), lambda bi, ti: (bi, ti, 4)), _full((1, HG_DV)),
                  hm, hm, hm, hm, _full((1, RW_WIDTH)), _full((1, RW_WIDTH)), _full(ones.shape),
                  pl.BlockSpec(w_out.shape, lambda bi, ti: (0, 0), pipeline_mode=pl.Buffered(1))],
        out_specs=tok(d),
        out_shape=jax.ShapeDtypeStruct(x.shape, F32),
        compiler_params=_params(("parallel", "parallel")),
    )(x, g1.reshape(b, 1, d), o_f, o_b, p_hg, hg_g.reshape(1, HG_DV), rw_f, rw_b, bonus, gate,
      ln_g.reshape(1, RW_WIDTH), ln_b.reshape(1, RW_WIDTH), jnp.asarray(ones, BF16), w_out.astype(BF16))


def _cpow_table(zr, zi, n):
    def step(c, _):
        cr, ci = c
        return (cr * zr - ci * zi, cr * zi + ci * zr), (cr, ci)
    (_, _), (pr, pi) = lax.scan(step, (jnp.ones_like(zr), jnp.zeros_like(zr)), None, length=n)
    return pr, pi


def _s5_params(a_re, a_im, log_dt, b_re, b_im, c_re, c_im, rev, nsteps):
    ell = S5_CHUNK
    dt = jnp.exp(log_dt)[:, None]
    mag = jnp.exp(a_re * dt)
    lr, li = mag * jnp.cos(a_im * dt), mag * jnp.sin(a_im * dt)
    den = a_re * a_re + a_im * a_im
    fr = ((lr - 1.0) * a_re + li * a_im) / den
    fi = (li * a_re - (lr - 1.0) * a_im) / den
    bbr = fr[..., None] * b_re - fi[..., None] * b_im
    bbi = fr[..., None] * b_im + fi[..., None] * b_re
    pr, pi = _cpow_table(lr, li, ell + 1)
    cr_p = c_re[None] * pr[:, :, None, :] - c_im[None] * pi[:, :, None, :]
    ci_p = c_re[None] * pi[:, :, None, :] + c_im[None] * pr[:, :, None, :]
    kern = jnp.einsum('tgcn,gni->tgci', cr_p[:ell], bbr) - jnp.einsum('tgcn,gni->tgci', ci_p[:ell], bbi)
    s_idx = np.arange(ell)[:, None]
    t_idx = np.arange(ell)[None, :]
    lag = (s_idx - t_idx) if rev else (t_idx - s_idx)
    valid = jnp.asarray(lag >= 0)
    kt = kern[np.clip(lag, 0, ell - 1)]
    kt = jnp.where(valid[:, :, None, None, None], kt, 0.0)
    toep = jnp.transpose(kt, (2, 0, 4, 1, 3)).reshape(S5_GROUPS, S5_FEAT, S5_FEAT)
    e_idx = (np.arange(ell) if rev else (ell - 1 - np.arange(ell)))
    pwr, pwi = pr[e_idx], pi[e_idx]
    p_re = pwr[..., None] * bbr[None] - pwi[..., None] * bbi[None]
    p_im = pwr[..., None] * bbi[None] + pwi[..., None] * bbr[None]
    p_re = jnp.transpose(p_re, (1, 0, 3, 2)).reshape(S5_GROUPS, S5_FEAT, S5_STATE)
    p_im = jnp.transpose(p_im, (1, 0, 3, 2)).reshape(S5_GROUPS, S5_FEAT, S5_STATE)
    q_idx = ((ell - np.arange(ell)) if rev else (np.arange(ell) + 1))
    q_re = jnp.transpose(cr_p[q_idx], (1, 3, 0, 2)).reshape(S5_GROUPS, S5_STATE, S5_FEAT)
    q_im = -jnp.transpose(ci_p[q_idx], (1, 3, 0, 2)).reshape(S5_GROUPS, S5_STATE, S5_FEAT)
    zr, zi = [pr[ell]], [pi[ell]]
    for _ in range(nsteps - 1):
        zr, zi = zr + [zr[-1] * zr[-1] - zi[-1] * zi[-1]], zi + [2.0 * zr[-1] * zi[-1]]
    zr = jnp.stack(zr)[:, :, None, :]
    zi = jnp.stack(zi)[:, :, None, :]
    return toep, p_re, p_im, q_re, q_im, zr, zi


def _s5_kernel(u_ref, toep_ref, pre_ref, pim_ref, qre_ref, qim_ref, zr_ref, zi_ref, h0r_ref, h0i_ref,
               y_ref, hfr_ref, hfi_ref, cr_ref, ci_ref, *, rev, nsteps):
    t = pl.program_id(2)

    @pl.when(t == 0)
    def _():
        cr_ref[...] = h0r_ref[0, 0]
        ci_ref[...] = h0i_ref[0, 0]

    u = u_ref[0]
    jb = u.shape[0]
    row = lax.broadcasted_iota(jnp.int32, (jb, 1), 0)
    edge = (jb - 1) if rev else 0
    xr = mm1(u, pre_ref[0])
    xi = mm1(u, pim_ref[0])
    car_r, car_i = cr_ref[...], ci_ref[...]
    z1r, z1i = zr_ref[0, 0], zi_ref[0, 0]
    at_edge = row == edge
    hr = xr + jnp.where(at_edge, z1r * car_r - z1i * car_i, 0.0)
    hi = xi + jnp.where(at_edge, z1r * car_i + z1i * car_r, 0.0)
    for s in range(nsteps):
        sh = 1 << s
        if sh >= jb:
            break
        zr, zi = zr_ref[s, 0], zi_ref[s, 0]
        if rev:
            pr_ = jnp.where(row < jb - sh, pltpu.roll(hr, jb - sh, axis=0), 0.0)
            pi_ = jnp.where(row < jb - sh, pltpu.roll(hi, jb - sh, axis=0), 0.0)
        else:
            pr_ = jnp.where(row >= sh, pltpu.roll(hr, sh, axis=0), 0.0)
            pi_ = jnp.where(row >= sh, pltpu.roll(hi, sh, axis=0), 0.0)
        hr, hi = hr + (zr * pr_ - zi * pi_), hi + (zr * pi_ + zi * pr_)
    if rev:
        hpr = jnp.where(at_edge, car_r, pltpu.roll(hr, jb - 1, axis=0))
        hpi = jnp.where(at_edge, car_i, pltpu.roll(hi, jb - 1, axis=0))
        cr_ref[...] = hr[0:1]
        ci_ref[...] = hi[0:1]
    else:
        hpr = jnp.where(at_edge, car_r, pltpu.roll(hr, 1, axis=0))
        hpi = jnp.where(at_edge, car_i, pltpu.roll(hi, 1, axis=0))
        cr_ref[...] = hr[jb - 1:jb]
        ci_ref[...] = hi[jb - 1:jb]
    y_ref[0] = mm1(u, toep_ref[0]) + mm1(hpr, qre_ref[0]) + mm1(hpi, qim_ref[0])

    @pl.when(t == pl.num_programs(2) - 1)
    def _():
        hfr_ref[0, 0] = cr_ref[...]
        hfi_ref[0, 0] = ci_ref[...]


def s5_dir(rev, ug, prm, h0r, h0i, nb, jb=128):
    g, rows, feat = ug.shape
    j = rows // nb
    jb = min(jb, j)
    nj = j // jb
    nsteps = max(int(math.log2(jb)), 1)
    toep, p_re, p_im, q_re, q_im, zr, zi = _s5_params(*prm, rev=rev, nsteps=nsteps)

    def rmap(bi, ti):
        return bi * nj + ((nj - 1 - ti) if rev else ti)

    gspec = lambda s: pl.BlockSpec((1,) + s, lambda gi, bi, ti: (gi, 0, 0))
    hspec = pl.BlockSpec((1, 1, 1, S5_STATE), lambda gi, bi, ti: (gi, bi, 0, 0))
    zspec = pl.BlockSpec((nsteps, 1, 1, S5_STATE), lambda gi, bi, ti: (0, gi, 0, 0))
    y, hfr, hfi = pl.pallas_call(
        functools.partial(_s5_kernel, rev=rev, nsteps=nsteps),
        grid=(g, nb, nj),
        in_specs=[pl.BlockSpec((1, jb, feat), lambda gi, bi, ti: (gi, rmap(bi, ti), 0)),
                  gspec((feat, feat)), gspec((feat, S5_STATE)), gspec((feat, S5_STATE)),
                  gspec((S5_STATE, feat)), gspec((S5_STATE, feat)), zspec, zspec, hspec, hspec],
        out_specs=[pl.BlockSpec((1, jb, feat), lambda gi, bi, ti: (gi, rmap(bi, ti), 0)), hspec, hspec],
        out_shape=[jax.ShapeDtypeStruct(ug.shape, F32), jax.ShapeDtypeStruct(h0r.shape, F32),
                   jax.ShapeDtypeStruct(h0i.shape, F32)],
        scratch_shapes=[pltpu.VMEM((1, S5_STATE), F32), pltpu.VMEM((1, S5_STATE), F32)],
        compiler_params=_params(("parallel", "parallel", "arbitrary")),
    )(ug, toep.astype(BF16), p_re.astype(BF16), p_im.astype(BF16), q_re.astype(BF16), q_im.astype(BF16),
      zr, zi, h0r, h0i)
    return y, hfr, hfi


def _fold_chunks(u):
    b, t, _ = u.shape
    z = u.reshape(b, t // S5_CHUNK, S5_CHUNK, S5_GROUPS, S5_GROUP)
    return jnp.transpose(z, (3, 0, 1, 2, 4)).reshape(S5_GROUPS, b * (t // S5_CHUNK), S5_FEAT)


def _unfold_chunks(y, b):
    g, rows, _ = y.shape
    j = rows // b
    z = y.reshape(g, b, j, S5_CHUNK, S5_GROUP)
    return jnp.transpose(z, (1, 2, 3, 0, 4)).reshape(b, j * S5_CHUNK, S5_WIDTH)


def _gelu_tanh(x):
    return 0.5 * x * (1.0 + jnp.tanh(math.sqrt(2.0 / math.pi) * (x + 0.044715 * (x * x * x))))


def _odd_out_kernel(x_ref, g1_ref, of_ref, ob_ref, gg_ref, ng_ref, yf_ref, yb_ref, u_ref, dsk_ref,
                    wglu_ref, bglu_ref, w_ref, o_ref):
    z = of_ref[0] + ob_ref[0]
    gg = gg_ref[0]
    acc = None
    for h in range(GLA_HEADS):
        ls = slice(h * GLA_DV, (h + 1) * GLA_DV)
        zh = z[:, ls]
        yh = zh * lax.rsqrt(jnp.mean(zh * zh, axis=-1, keepdims=True) + EPS) * ng_ref[...] * _silu(gg[:, ls])
        part = mm1(yh, w_ref[ls, :])
        acc = part if acc is None else acc + part
    y = (yf_ref[0] + yb_ref[0]) + dsk_ref[...] * u_ref[0]
    zz = _gelu_tanh(y)
    o_s5 = zz * jax.nn.sigmoid(mm1(zz, wglu_ref[...]) + bglu_ref[...])
    acc = acc + mm1(o_s5, w_ref[GLA_WIDTH:, :])
    o_ref[0] = x_ref[0] + g1_ref[0] * acc


def odd_out(x, g1, o_f, o_b, g_gla, gla_g, y_f, y_b, u, d_skip, w_glu, b_glu, w_out, tm=256):
    b, t, d = x.shape
    tm = min(tm, t)
    tok = lambda w: pl.BlockSpec((1, tm, w), lambda bi, ti: (bi, ti, 0))
    const = lambda a: pl.BlockSpec(a.shape, lambda bi, ti: (0, 0), pipeline_mode=pl.Buffered(1))
    return pl.pallas_call(
        _odd_out_kernel,
        grid=(b, t // tm),
        in_specs=[tok(d), pl.BlockSpec((1, 1, d), lambda bi, ti: (bi, 0, 0)), tok(GLA_WIDTH), tok(GLA_WIDTH),
                  tok(GLA_WIDTH), _full((1, GLA_DV)), tok(S5_WIDTH), tok(S5_WIDTH), tok(S5_WIDTH),
                  _full((1, S5_WIDTH)), const(w_glu), _full((1, S5_WIDTH)), const(w_out)],
        out_specs=tok(d),
        out_shape=jax.ShapeDtypeStruct(x.shape, F32),
        compiler_params=_params(("parallel", "parallel")),
    )(x, g1.reshape(b, 1, d), o_f, o_b, g_gla, gla_g.reshape(1, GLA_DV), y_f, y_b, u,
      d_skip.reshape(1, S5_WIDTH), w_glu.astype(BF16), b_glu.reshape(1, S5_WIDTH), w_out.astype(BF16))


def _ffn_kernel(x_ref, xp_ref, xn_ref, ng_ref, sc_ref, sh_ref, g2_ref, wup_ref, cw_ref, cb_ref, wdn_ref,
                fg_ref, o_ref, *, cols, fw, final):
    r = pl.program_id(1)
    nr = pl.num_programs(1)
    x = x_ref[0]
    n = x.shape[0]

    def modnorm(z):
        h = z * lax.rsqrt(jnp.mean(z * z, axis=-1, keepdims=True) + EPS) * ng_ref[...]
        return (h * (1.0 + sc_ref[0]) + sh_ref[0]).astype(BF16)

    h_main = modnorm(x)
    h_all = jnp.concatenate([modnorm(xp_ref[0]), h_main, modnorm(xn_ref[0])], axis=0)
    na = n + 2 * cols
    pos = lax.broadcasted_iota(jnp.int32, (na, 1), 0)
    colid = jnp.bitwise_and(pos, cols - 1)
    lo = jnp.where(r > 0, 0, cols)
    hi = jnp.where(r < nr - 1, na, cols + n)
    valid = (pos >= lo) & (pos < hi)
    acc = None
    for f0 in range(0, D_FF, fw):
        gate = _dot(h_all, wup_ref[:, f0:f0 + fw])
        gate = jnp.where(valid, gate, 0.0)
        val = _dot(h_main, wup_ref[:, D_FF + f0:D_FF + f0 + fw])
        g_l = jnp.where(colid == 0, 0.0, pltpu.roll(gate, 1, axis=0))
        g_r = jnp.where(colid == cols - 1, 0.0, pltpu.roll(gate, na - 1, axis=0))
        conv = cb_ref[:, f0:f0 + fw]
        for di in range(3):
            rs = slice(di * cols, di * cols + n)
            conv = conv + (g_l[rs] * cw_ref[3 * di + 0:3 * di + 1, f0:f0 + fw]
                           + gate[rs] * cw_ref[3 * di + 1:3 * di + 2, f0:f0 + fw]
                           + g_r[rs] * cw_ref[3 * di + 2:3 * di + 3, f0:f0 + fw])
        act = (_silu(conv) * val).astype(BF16)
        part = _dot(act, wdn_ref[f0:f0 + fw, :])
        acc = part if acc is None else acc + part
    y = x + g2_ref[0] * acc
    if final:
        y = y * lax.rsqrt(jnp.mean(y * y, axis=-1, keepdims=True) + EPS) * fg_ref[...]
    o_ref[0] = y


def conv_ffn(x, ng, sc, sh, g2, w_up, conv_w, conv_b, w_down, final_g, rows, cols, final, rt=8, fw=256):
    b, t, d = x.shape
    rt = min(rt, rows)
    nr = rows // rt
    n = rt * cols
    vec = lambda a: pl.BlockSpec((1, 1, d), lambda bi, ri: (bi, 0, 0))
    const = lambda a: pl.BlockSpec(a.shape, lambda bi, ri: (0, 0), pipeline_mode=pl.Buffered(1))
    wup = w_up.astype(BF16)
    wdn = w_down.astype(BF16)
    cw = conv_w.reshape(9, D_FF)
    cb = conv_b.reshape(1, D_FF)
    return pl.pallas_call(
        functools.partial(_ffn_kernel, cols=cols, fw=fw, final=final),
        grid=(b, nr),
        in_specs=[pl.BlockSpec((1, n, d), lambda bi, ri: (bi, ri, 0)),
                  pl.BlockSpec((1, cols, d), lambda bi, ri: (bi, jnp.maximum(ri * rt - 1, 0), 0)),
                  pl.BlockSpec((1, cols, d), lambda bi, ri: (bi, jnp.minimum((ri + 1) * rt, rows - 1), 0)),
                  _full((1, d)), vec(sc), vec(sh), vec(g2), const(wup), _full((9, D_FF)), _full((1, D_FF)),
                  const(wdn), _full((1, d))],
        out_specs=pl.BlockSpec((1, n, d), lambda bi, ri: (bi, ri, 0)),
        out_shape=jax.ShapeDtypeStruct(x.shape, F32),
        compiler_params=_params(("parallel", "parallel")),
    )(x, x, x, ng.reshape(1, d), sc.reshape(b, 1, d), sh.reshape(b, 1, d), g2.reshape(b, 1, d), wup, cw, cb,
      wdn, final_g.reshape(1, d))


def _even_layer(hx_args, ctx_args, prm, with_ctx_out):
    (w_in, w_out, lb, hg_g, mu, w0, w2, a0, a2, g2, k_k, k_a, r_k, ln_g, ln_b) = prm
    hg_cols = 5 * HG_WIDTH
    res = {}
    bsz = hx_args[0].shape[0]
    s_hg = [jnp.zeros((bsz, HG_HEADS, HG_DV, LANES), F32)] * 2
    s_rw = [jnp.zeros((bsz, RW_WIDTH // LANES, RW_DH, LANES), F32)] * 2
    for name, (x, g, sc, sh, g1), want_out in (("ctx", ctx_args, with_ctx_out), ("x", hx_args, True)):
        p_hg, p_rw = norm_proj(x, g, sc, sh, w_in, (hg_cols, RW_COLS))
        r, v, kk, gate, bonus, lw, kd, kb = rwkv_prep(p_rw, mu, w0, w2, a0, a2, g2, k_k, k_a, r_k)
        o_hg, o_rw = [], []
        for d in range(2):
            o, s_hg[d] = gated_scan("hgrn", d == 1, (p_hg, lb[d:d + 1], d), s_hg[d])
            o_hg.append(o)
            o, s_rw[d] = rwkv_scan(d == 1, d, r, v, kk, lw, kd, kb, s_rw[d])
            o_rw.append(o)
        if want_out:
            res[name] = even_out(x, g1, o_hg[0], o_hg[1], p_hg, hg_g, o_rw[0], o_rw[1], bonus, gate,
                                 ln_g, ln_b, w_out)
    return res


def _odd_layer(hx_args, ctx_args, prm, with_ctx_out):
    (w_in, w_out, gla_w2, gla_b, gla_g, a_re, a_im, log_dt, b_re, b_im, c_re, c_im, d_skip, w_glu, b_glu) = prm
    nq = 2 * GLA_QK + 2 * GLA_WIDTH
    w_perm = jnp.concatenate([w_in[:, :nq], w_in[:, nq + 2 * GLA_GATE_LORA:], w_in[:, nq:nq + 2 * GLA_GATE_LORA]], 1)
    z16 = jnp.zeros((GLA_GATE_LORA, GLA_QK), F32)
    w2pad = [jnp.concatenate([gla_w2[0], z16], 0).astype(BF16), jnp.concatenate([z16, gla_w2[1]], 0).astype(BF16)]
    res = {}
    bsz = hx_args[0].shape[0]
    s_gla = [jnp.zeros((bsz, GLA_HEADS, GLA_DV, LANES), F32)] * 2
    zs = jnp.zeros((S5_GROUPS, bsz, 1, S5_STATE), F32)
    s_s5 = [(zs, zs), (zs, zs)]
    for name, (x, g, sc, sh, g1), want_out in (("ctx", ctx_args, with_ctx_out), ("x", hx_args, True)):
        q, k, v, gg, u, gd = norm_proj(x, g, sc, sh, w_perm,
                                       (GLA_QK, GLA_QK, GLA_WIDTH, GLA_WIDTH, S5_WIDTH, 2 * GLA_GATE_LORA))
        ug = _fold_chunks(u)
        o_gla, y_s5 = [], []
        for d in range(2):
            o, s_gla[d] = gated_scan("gla", d == 1, (q, k, v, gd, w2pad[d], gla_b[d].reshape(1, GLA_QK)), s_gla[d])
            o_gla.append(o)
            prm_d = (a_re[d], a_im[d], log_dt[d], b_re[d], b_im[d], c_re[d], c_im[d])
            y, hr, hi = s5_dir(d == 1, ug, prm_d, s_s5[d][0], s_s5[d][1], bsz)
            s_s5[d] = (hr, hi)
            y_s5.append(_unfold_chunks(y, bsz))
        if want_out:
            res[name] = odd_out(x, g1, o_gla[0], o_gla[1], gg, gla_g, y_s5[0], y_s5[1], u, d_skip, w_glu, b_glu,
                                w_out)
    return res


def kernel(x, c, ctx, c_ctx, ada_w, ada_b, norm1_g, norm2_g, final_g,
           ev_w_in, ev_w_out, hg_lb, hg_norm_g, rw_mu, rw_w0, rw_w2, rw_a0, rw_a2, rw_g2,
           rw_k_k, rw_k_a, rw_r_k, rw_ln_g, rw_ln_b,
           od_w_in, od_w_out, gla_w2, gla_b, gla_norm_g, s5_a_re, s5_a_im, s5_log_dt,
           s5_b_re, s5_b_im, s5_c_re, s5_c_im, s5_d, s5_w_glu, s5_b_glu,
           ffn_w_up, ffn_conv_w, ffn_conv_b, ffn_w_down):
    bsz, seq, d = x.shape
    depth = ada_w.shape[0]
    rows = seq // GRID_W
    ctx_len = ctx.shape[1]
    lb_all = jnp.cumsum(jax.nn.softmax(hg_lb.astype(F32), axis=0), axis=0)

    for layer in range(depth):
        last = layer == depth - 1
        j = layer // 2
        mod = jnp.split(jax.nn.silu(c) @ ada_w[layer] + ada_b[layer], 6, axis=-1)
        mod_c = jnp.split(jax.nn.silu(c_ctx) @ ada_w[layer] + ada_b[layer], 6, axis=-1)
        sh1, sc1, g1, sh2, sc2, g2 = mod
        csh1, csc1, cg1, csh2, csc2, cg2 = [jnp.broadcast_to(m[None], (bsz, d)) for m in mod_c]
        hx_args = (x, norm1_g[layer], sc1, sh1, g1)
        ctx_args = (ctx, norm1_g[layer], csc1, csh1, cg1)
        if layer % 2 == 0:
            prm = (ev_w_in[j], ev_w_out[j], lb_all[j], hg_norm_g[j], rw_mu[j], rw_w0[j], rw_w2[j], rw_a0[j],
                   rw_a2[j], rw_g2[j], rw_k_k[j], rw_k_a[j], rw_r_k[j], rw_ln_g[j], rw_ln_b[j])
            res = _even_layer(hx_args, ctx_args, prm, not last)
        else:
            prm = (od_w_in[j], od_w_out[j], gla_w2[j], gla_b[j], gla_norm_g[j], s5_a_re[j], s5_a_im[j],
                   s5_log_dt[j], s5_b_re[j], s5_b_im[j], s5_c_re[j], s5_c_im[j], s5_d[j], s5_w_glu[j], s5_b_glu[j])
            res = _odd_layer(hx_args, ctx_args, prm, not last)
        ffn = (ffn_w_up[layer], ffn_conv_w[layer], ffn_conv_b[layer], ffn_w_down[layer], final_g)
        x = conv_ffn(res["x"], norm2_g[layer], sc2, sh2, g2, *ffn, rows, GRID_W, final=last)
        if not last:
            ctx = conv_ffn(res["ctx"], norm2_g[layer], csc2, csh2, cg2, *ffn, 1, ctx_len, final=False)
    return x
```

```python
import functools
import math

import numpy as np
import jax
import jax.numpy as jnp
from jax import lax
from jax.experimental import pallas as pl
from jax.experimental.pallas import tpu as pltpu

F32 = jnp.float32
BF16 = jnp.bfloat16

EPS = 1e-6
GRID_W = 64
CHUNK = 64
LEVELS = 6

HG_HEADS, HG_DK, HG_DV = 4, 128, 128
HG_WIDTH = HG_HEADS * HG_DV
RW_HEADS, RW_DH = 8, 64
RW_WIDTH = RW_HEADS * RW_DH
RW_DECAY_LORA, RW_AAA_LORA, RW_GATE_LORA = 64, 64, 128
RW_GN_EPS = 64e-5
RW_COLS = 3 * RW_WIDTH + 2 * RW_DECAY_LORA + 2 * RW_AAA_LORA + RW_GATE_LORA
GLA_HEADS, GLA_DK, GLA_DV = 4, 64, 128
GLA_WIDTH = GLA_HEADS * GLA_DV
GLA_QK = GLA_HEADS * GLA_DK
GLA_GATE_LORA = 16
GLA_TAU = 16.0
S5_WIDTH, S5_GROUP, S5_STATE = 512, 16, 64
S5_GROUPS = S5_WIDTH // S5_GROUP
S5_CHUNK = 16
S5_FEAT = S5_CHUNK * S5_GROUP
S5_ROWS = 128
S5_MAX_STEPS = S5_ROWS.bit_length() - 1
D_FF = 2816

LANES = 128
VMEM_LIMIT = 56 * 1024 * 1024

NN = ((1,), (0,))
NT = ((1,), (1,))
TN = ((0,), (0,))


def _dot(a, b, dims=NN):
    return lax.dot_general(a, b, (dims, ((), ())), preferred_element_type=F32)


def _split(a):
    hi = a.astype(BF16)
    return hi, (a - hi.astype(F32)).astype(BF16)


def mm1(a, b, dims=NN):
    return _dot(a.astype(BF16), b.astype(BF16), dims)


def mm3(a, b, dims=NN):
    ah, al = _split(a)
    bh, bl = _split(b)
    return _dot(ah, bh, dims) + (_dot(al, bh, dims) + _dot(ah, bl, dims))


def mmc(c, x, dims=NN):
    xh, xl = _split(x)
    return _dot(c, xh, dims) + _dot(c, xl, dims)


def mmcr(x, c, dims=NN):
    xh, xl = _split(x)
    return _dot(xh, c, dims) + _dot(xl, c, dims)


def _silu(x):
    return x * jax.nn.sigmoid(x)


def _params(sem):
    return pltpu.CompilerParams(dimension_semantics=sem, vmem_limit_bytes=VMEM_LIMIT)


def _full(shape):
    nd = len(shape)
    return pl.BlockSpec(shape, lambda *_: (0,) * nd)


def _tau(rev):
    t = np.arange(CHUNK)
    return (CHUNK - 1 - t) if rev else t


def _level_masks(rev):
    tau = _tau(rev)
    ti, si = tau[:, None], tau[None, :]
    out = []
    for l in range(LEVELS):
        m = (((ti >> l) & 1) == 1) & (((si >> l) & 1) == 0) & ((ti >> (l + 1)) == (si >> (l + 1)))
        out.append(m)
    return np.stack(out).astype(np.float32)


def _gls_consts(rev):
    tau = _tau(rev)
    ti, ii = tau[:, None], tau[None, :]
    mats = [ii <= ti, ii > ti]
    sel = []
    for l in range(LEVELS):
        bit = (ti >> l) & 1
        start = (ti >> l) << l
        end = start + (1 << l) - 1
        mats.append(np.where(bit == 1, (ii >= start) & (ii <= ti), (ii > ti) & (ii <= end)))
        sel.append(np.broadcast_to(bit, (CHUNK, LANES)))
    mats.append(np.ones((8, CHUNK), bool))
    cstack = np.concatenate(mats, 0).astype(np.float32)
    return (jnp.asarray(cstack, BF16), jnp.asarray(np.stack(sel).astype(np.float32)),
            jnp.asarray(_level_masks(rev)), jnp.asarray(np.eye(CHUNK, dtype=np.float32)))


def _rwkv_consts(rev):
    tau = _tau(rev)
    ti, ii = tau[:, None], tau[None, :]
    cum = np.concatenate([ii <= ti, ii < ti, ii > ti, np.ones((8, CHUNK), bool)], 0).astype(np.float32)
    tri = np.stack([ii < ti, ii <= ti]).astype(np.float32)
    pair = lambda m: np.concatenate([m, m], axis=-1)
    hmask = np.zeros((2, LANES), np.float32)
    hmask[0, :RW_DH] = 1.0
    hmask[1, RW_DH:] = 1.0
    return (jnp.asarray(cum, BF16), jnp.asarray(pair(tri)), jnp.asarray(pair(_level_masks(rev))),
            jnp.asarray(pair(np.eye(CHUNK, dtype=np.float32))), jnp.asarray(hmask))


def _norm_proj_kernel(x_ref, g_ref, sc_ref, sh_ref, w_ref, *o_refs, splits, blocked):
    x = x_ref[...]
    h = x * lax.rsqrt(jnp.mean(x * x, axis=-1, keepdims=True) + EPS) * g_ref[...]
    h = (h * (1.0 + sc_ref[0]) + sh_ref[0]).astype(BF16)
    off = 0
    for i, (o_ref, n) in enumerate(zip(o_refs, splits)):
        res = _dot(h, w_ref[:, off:off + n])
        if i in blocked:
            for c in range(n // LANES):
                o_ref[c] = res[:, c * LANES:(c + 1) * LANES]
        else:
            o_ref[...] = res
        off += n


def norm_proj(x, g, sc, sh, w, splits, blocked=(), tm=256):
    b, t, d = x.shape
    tm = min(tm, t)
    steps = t // tm
    n = w.shape[1]
    out_specs, out_shape = [], []
    for i, s in enumerate(splits):
        if i in blocked:
            out_specs.append(pl.BlockSpec((s // LANES, tm, LANES), lambda i: (0, i, 0)))
            out_shape.append(jax.ShapeDtypeStruct((s // LANES, b * t, LANES), F32))
        else:
            out_specs.append(pl.BlockSpec((tm, s), lambda i: (i, 0)))
            out_shape.append(jax.ShapeDtypeStruct((b * t, s), F32))
    outs = pl.pallas_call(
        functools.partial(_norm_proj_kernel, splits=tuple(splits), blocked=tuple(blocked)),
        grid=(b * steps,),
        in_specs=[pl.BlockSpec((tm, d), lambda i: (i, 0)),
                  _full((1, d)),
                  pl.BlockSpec((1, 1, d), lambda i: (i // steps, 0, 0)),
                  pl.BlockSpec((1, 1, d), lambda i: (i // steps, 0, 0)),
                  pl.BlockSpec((d, n), lambda i: (0, 0), pipeline_mode=pl.Buffered(1))],
        out_specs=out_specs,
        out_shape=out_shape,
        compiler_params=_params(("parallel",)),
    )(x.reshape(b * t, d), g.reshape(1, d), sc.reshape(b, 1, d), sh.reshape(b, 1, d), w.astype(BF16))
    return [o.reshape(s // LANES, b, t, LANES) if i in blocked else o.reshape(b, t, s)
            for i, (o, s) in enumerate(zip(outs, splits))]


def _gls_chunks(groups, heads, cstack, sel, lmask, eye):
    ell = CHUNK
    c2 = jnp.concatenate([cstack, cstack], axis=1)
    es = [jnp.exp(_dot(c2, jnp.concatenate(_split(g), axis=0))) for _, _, g in groups]
    xs = [[e[(2 + l) * ell:(3 + l) * ell] * jnp.where(sel[l] > 0.0, q, k) for l in range(LEVELS)]
          for (q, k, _), e in zip(groups, es)]

    def msk(z, lm):
        return z if lm is None else z * lm

    att = [mm1(msk(groups[gi][0], lm), groups[gi][1], NT) * eye for gi, _, lm in heads]
    for l in range(LEVELS):
        att = [a + mm1(msk(xs[gi][l], lm), xs[gi][l], NT) * lmask[l] for a, (gi, _, lm) in zip(att, heads)]
    ov = [mm1(a, v) for a, (_, v, _) in zip(att, heads)]
    qb = [msk(groups[gi][0] * es[gi][0:ell], lm) for gi, _, lm in heads]
    kv = [mm1(v, msk(groups[gi][1] * es[gi][ell:2 * ell], lm), TN) for gi, v, lm in heads]
    dec = [e[8 * ell:8 * ell + 1] for e in es]
    return ov, qb, kv, dec


def _gls_kernel(*refs, mode, rev, nsub, nchunk, cu):
    if mode == "hgrn":
        (q_ref, v_ref, f_ref, lb_ref, s0_ref, cst_ref, sel_ref, lm_ref, eye_ref,
         o_ref, sfin_ref, st_ref) = refs
    else:
        (q_ref, k_ref, v_ref, gd_ref, w2_ref, gb_ref, hm_ref, s0_ref, cst_ref, sel_ref, lm_ref, eye_ref,
         o_ref, sfin_ref, st_ref) = refs
    t = pl.program_id(1)
    ngroups = q_ref.shape[-1] // LANES

    @pl.when(t == 0)
    def _():
        st_ref[...] = s0_ref[0]

    cstack = cst_ref[...]
    sel = [sel_ref[l] for l in range(LEVELS)]
    lmask = [lm_ref[l] for l in range(LEVELS)]
    eye = eye_ref[...]

    def body(it, carry):
        nh = ngroups * nsub
        state = [st_ref[h] for h in range(nh)]
        rows, loaded = [], []
        for j in range(cu):
            ci = it * cu + j
            c = (nchunk - 1 - ci) if rev else ci
            rows.append(pl.ds(pl.multiple_of(c * CHUNK, CHUNK), CHUNK))
            if mode == "hgrn":
                loaded.append((q_ref[0, rows[j], :], f_ref[0, rows[j], :], v_ref[0, rows[j], :]))
            else:
                loaded.append((q_ref[0, rows[j], :], k_ref[0, rows[j], :], gd_ref[0, rows[j], :],
                               v_ref[0, rows[j], :]))
        groups, heads = [], []
        for j in range(cu):
            if mode == "hgrn":
                qx, fx, v_all = loaded[j]
                q_all = _silu(qx) * (HG_DK ** -0.5)
                lb = lb_ref[...]
                f = lb + (1.0 - lb) * jax.nn.sigmoid(fx)
                k_all = 1.0 - f
                g_all = jnp.log(f)
            else:
                qx, k_all, gdx, v_all = loaded[j]
                q_all = qx * (GLA_DK ** -0.5)
                z = mm1(gdx, w2_ref[...]) + gb_ref[...]
                g_all = (jnp.minimum(z, 0.0) - jnp.log1p(jnp.exp(-jnp.abs(z)))) * (1.0 / GLA_TAU)
            for gi in range(ngroups):
                ls = slice(gi * LANES, (gi + 1) * LANES)
                groups.append((q_all[:, ls], k_all[:, ls], g_all[:, ls]))
                for i in range(nsub):
                    h = gi * nsub + i
                    heads.append((j * ngroups + gi, v_all[:, h * LANES:(h + 1) * LANES],
                                  None if nsub == 1 else hm_ref[i:i + 1, :]))
        ov, qb, kv, dec = _gls_chunks(groups, heads, cstack, sel, lmask, eye)
        results = []
        for idx, (gidx, _, lm) in enumerate(heads):
            j, h = idx // nh, idx % nh
            results.append((j, h, ov[idx] + mm1(qb[idx], state[h], NT)))
            st_new = state[h] * dec[gidx] + kv[idx]
            state[h] = st_new if lm is None else st_new * lm
        for j, h, o in results:
            o_ref[0, rows[j], h * LANES:(h + 1) * LANES] = o
        for h in range(nh):
            st_ref[h] = state[h]
        return carry

    lax.fori_loop(0, nchunk // cu, body, 0)

    @pl.when(t == pl.num_programs(1) - 1)
    def _():
        sfin_ref[0] = st_ref[...]


def gated_scan(mode, rev, arrays, s0, tb=512, cu=2):
    consts = _gls_consts(rev)
    if mode == "hgrn":
        p_hg, lb, d = arrays
        b, t, _ = p_hg.shape
        nh, nsub, width = HG_HEADS, 1, HG_WIDTH
    else:
        q, k, v, gd, w2pad, gb = arrays
        b, t, _ = q.shape
        nh, nsub, width = GLA_HEADS, 2, GLA_WIDTH
    tb = min(tb, t)
    nt = t // tb
    nchunk = tb // CHUNK

    def tmap(i):
        return (nt - 1 - i) if rev else i

    def col(c, w):
        return pl.BlockSpec((1, tb, w), lambda bi, ti: (bi, tmap(ti), c))

    st_spec = pl.BlockSpec((1, nh, s0.shape[2], LANES), lambda bi, ti: (bi, 0, 0, 0))
    cspecs = [_full(c.shape) for c in consts]
    if mode == "hgrn":
        ins = [p_hg, p_hg, p_hg, lb, s0, *consts]
        in_specs = [col(0, width), col(1, width), col(2 + d, width), _full((1, width)), st_spec, *cspecs]
    else:
        hm = np.zeros((2, LANES), np.float32)
        hm[0, :GLA_DK] = 1.0
        hm[1, GLA_DK:] = 1.0
        ins = [q, k, v, gd, w2pad, gb, jnp.asarray(hm), s0, *consts]
        in_specs = [col(0, GLA_QK), col(0, GLA_QK), col(0, width), col(0, 2 * GLA_GATE_LORA),
                    _full(w2pad.shape), _full(gb.shape), _full((2, LANES)), st_spec, *cspecs]
    o, sfin = pl.pallas_call(
        functools.partial(_gls_kernel, mode=mode, rev=rev, nsub=nsub, nchunk=nchunk, cu=cu),
        grid=(b, nt),
        in_specs=in_specs,
        out_specs=[pl.BlockSpec((1, tb, width), lambda bi, ti: (bi, tmap(ti), 0)), st_spec],
        out_shape=[jax.ShapeDtypeStruct((b, t, width), F32), jax.ShapeDtypeStruct(s0.shape, F32)],
        scratch_shapes=[pltpu.VMEM(s0.shape[1:], F32)],
        compiler_params=_params(("parallel", "arbitrary")),
    )(*ins)
    return o, sfin


def _rwkv_prep_kernel(p_ref, pp_ref, pn_ref, mu_ref, w0_ref, w2_ref, a0_ref, a2_ref, g2_ref, kk_ref, ka_ref,
                      rk_ref, ones_ref, r_o, v_o, kk_o, gate_o, bonus_o, lw_o, kd_o, kb_o):
    t = pl.program_id(1)
    nt = pl.num_programs(1)
    p = p_ref[0]
    tb = p.shape[0]
    row = lax.broadcasted_iota(jnp.int32, (tb, 1), 0)
    prev_row = jnp.where(t > 0, pp_ref[0, 7:8, :], 0.0)
    next_row = jnp.where(t < nt - 1, pn_ref[0, 0:1, :], 0.0)
    prev = jnp.where(row == 0, prev_row, pltpu.roll(p, 1, axis=0))
    nxt = jnp.where(row == tb - 1, next_row, pltpu.roll(p, tb - 1, axis=0))
    s = p + mu_ref[0:1, :] * (prev - p) + mu_ref[1:2, :] * (nxt - p)
    w = RW_WIDTH
    r, k, v = s[:, 0:w], s[:, w:2 * w], s[:, 2 * w:3 * w]
    wd = jnp.tanh(s[:, 3 * w:3 * w + LANES])
    ad = s[:, 3 * w + LANES:3 * w + 2 * LANES]
    gd = s[:, 3 * w + 2 * LANES:3 * w + 3 * LANES]
    ones = ones_ref[...]
    kk = k * kk_ref[...]
    nrm = jnp.sqrt(mmcr(kk * kk, ones))
    kk = kk / jnp.maximum(nrm, 1e-12)
    gate = mm1(jax.nn.sigmoid(gd), g2_ref[...])
    ksum = jnp.zeros_like(k)
    for d in range(2):
        zw = w0_ref[d:d + 1, :] + mm1(wd, w2_ref[d])
        w_log = -(jnp.maximum(-zw, 0.0) + jnp.log1p(jnp.exp(-jnp.abs(zw)))) - 0.5
        lw = -jnp.exp(w_log)
        a = jax.nn.sigmoid(a0_ref[d:d + 1, :] + mm1(ad, a2_ref[d]))
        kd = k * (1.0 + (a - 1.0) * ka_ref[...])
        ksum = ksum + kd
        lw_o[d, 0] = lw
        kd_o[d, 0] = kd
        kb_o[d, 0] = kk * a
    r_o[0] = r
    v_o[0] = v
    kk_o[0] = kk
    gate_o[0] = gate
    bonus_o[0] = mmcr(r * ksum * rk_ref[...], ones) * v


def rwkv_prep(p_rw, mu, w0, w2, a0, a2, g2, k_k, k_a, r_k, tb=256):
    b, t, c = p_rw.shape
    tb = min(tb, t)
    nt = t // tb
    w = RW_WIDTH
    z = jnp.zeros((RW_DECAY_LORA, w), F32)
    w2p = jnp.stack([jnp.concatenate([w2[0], z], 0), jnp.concatenate([z, w2[1]], 0)]).astype(BF16)
    a2p = jnp.stack([jnp.concatenate([a2[0], z], 0), jnp.concatenate([z, a2[1]], 0)]).astype(BF16)
    ones = np.kron(np.eye(RW_HEADS, dtype=np.float32), np.ones((RW_DH, RW_DH), np.float32))
    nb8 = t // 8
    hm = jax.ShapeDtypeStruct((b, t, w), F32)
    hm2 = jax.ShapeDtypeStruct((2, b, t, w), F32)
    hm_spec = pl.BlockSpec((1, tb, w), lambda bi, ti: (bi, ti, 0))
    hm2_spec = pl.BlockSpec((2, 1, tb, w), lambda bi, ti: (0, bi, ti, 0))
    return pl.pallas_call(
        _rwkv_prep_kernel,
        grid=(b, nt),
        in_specs=[pl.BlockSpec((1, tb, c), lambda bi, ti: (bi, ti, 0)),
                  pl.BlockSpec((1, 8, c), lambda bi, ti: (bi, jnp.maximum(ti * (tb // 8) - 1, 0), 0)),
                  pl.BlockSpec((1, 8, c), lambda bi, ti: (bi, jnp.minimum((ti + 1) * (tb // 8), nb8 - 1), 0)),
                  _full((2, c)), _full((2, w)), _full((2, LANES, w)), _full((2, w)), _full((2, LANES, w)),
                  _full((RW_GATE_LORA, w)), _full((1, w)), _full((1, w)), _full((1, w)), _full((w, w))],
        out_specs=[hm_spec] * 5 + [hm2_spec] * 3,
        out_shape=[hm] * 5 + [hm2] * 3,
        compiler_params=_params(("parallel", "parallel")),
    )(p_rw, p_rw, p_rw, mu, w0, w2p, a0, a2p, g2.astype(BF16), k_k.reshape(1, w), k_a.reshape(1, w),
      r_k.reshape(1, w), jnp.asarray(ones, BF16))


def _bd(y, hm):
    return jnp.concatenate([y * hm[0], y * hm[1]], axis=0).astype(BF16)


def _bdiag(z, hm):
    return z[:CHUNK] * hm[0] + z[CHUNK:] * hm[1]


def _rwkv_chunks(ins, cum, tri, lmask, eye, hm):
    ell = CHUNK
    nc = range(len(ins))
    cum2 = jnp.concatenate([cum, cum], axis=1)
    cs = [_dot(cum2, jnp.concatenate(_split(x[5]), axis=0)) for x in ins]
    rt = [ins[i][0] * jnp.exp(cs[i][0:ell]) for i in nc]
    at = [ins[i][3] * jnp.exp(cs[i][ell:2 * ell]) for i in nc]
    e_neg = [jnp.exp(-cs[i][0:ell]) for i in nc]
    e_rem = [jnp.exp(cs[i][2 * ell:3 * ell]) for i in nc]
    e_tot = [jnp.exp(cs[i][3 * ell:3 * ell + 1]) for i in nc]
    ar = [jnp.concatenate([at[i], rt[i]], axis=0).astype(BF16) for i in nc]
    pb = [_dot(ar[i], _bd(ins[i][4] * e_neg[i], hm), NT) for i in nc]
    pk = [_dot(ar[i], _bd(ins[i][1] * e_neg[i], hm), NT) for i in nc]
    n = [pb[i][:ell] * tri[0] for i in nc]
    ti = [eye + n[i] * lmask[0] for i in nc]
    for l in range(1, LEVELS):
        p = [_dot((n[i] * lmask[l]).astype(BF16), _bd(ti[i], hm)) for i in nc]
        ti = [ti[i] + _dot(ti[i].astype(BF16), _bd(p[i], hm)) for i in nc]
    vbd = [_bd(ins[i][2], hm) for i in nc]
    mv = [_dot((pk[i][:ell] * tri[0]).astype(BF16), vbd[i]) for i in nc]
    au = [_dot(ti[i].astype(BF16), jnp.concatenate([_bd(at[i], hm), _bd(mv[i], hm)], axis=1)) for i in nc]
    ro = [_dot((pb[i][ell:] * tri[1]).astype(BF16),
               jnp.concatenate([_bd(au[i][:, :LANES], hm), _bd(au[i][:, LANES:], hm)], axis=1)) for i in nc]
    rk = [_dot((pk[i][ell:] * tri[1]).astype(BF16), vbd[i]) for i in nc]
    gh = [_dot((ins[i][4] * e_rem[i]).astype(BF16), au[i].astype(BF16), TN) for i in nc]
    kv = [_dot((ins[i][1] * e_rem[i]).astype(BF16), ins[i][2].astype(BF16), TN) for i in nc]
    out = []
    for i in nc:
        rhat = rt[i] + ro[i][:, :LANES]
        ohat = ro[i][:, LANES:] + rk[i]
        g = eye * e_tot[i] + _bdiag(gh[i][:, :LANES], hm)
        h = _bdiag(gh[i][:, LANES:], hm) + _bdiag(kv[i], hm)
        out.append((rhat, ohat, g, h))
    return out


def _rwkv_state_mm(x, st, hm):
    s_hi = st.astype(BF16).astype(F32)
    bdh, bdl = _bd(s_hi, hm), _bd(st - s_hi, hm)
    xh, xl = _split(x)
    return _dot(jnp.concatenate([xh, xl], axis=1), jnp.concatenate([bdh, bdh], axis=0)) + _dot(xh, bdl)


def _rwkv_kernel(r_ref, v_ref, a_ref, lw_ref, k_ref, b_ref, s0_ref, cum_ref, tri_ref, lm_ref, eye_ref, hm_ref,
                 o_ref, sfin_ref, st_ref, *, rev, nchunk, cu):
    t = pl.program_id(1)
    npair = r_ref.shape[-1] // LANES

    @pl.when(t == 0)
    def _():
        st_ref[...] = s0_ref[0]

    cum = cum_ref[...]
    tri = [tri_ref[0], tri_ref[1]]
    lmask = [lm_ref[l] for l in range(LEVELS)]
    eye = eye_ref[...]
    hm = [hm_ref[0:1, :], hm_ref[1:2, :]]

    def body(it, carry):
        sts = [st_ref[p] for p in range(npair)]
        rows, ins = [], []
        for j in range(cu):
            ci = it * cu + j
            c = (nchunk - 1 - ci) if rev else ci
            rows.append(pl.ds(pl.multiple_of(c * CHUNK, CHUNK), CHUNK))
            for p in range(npair):
                ls = slice(p * LANES, (p + 1) * LANES)
                ins.append((r_ref[0, rows[j], ls], k_ref[0, 0, rows[j], ls], v_ref[0, rows[j], ls],
                            -a_ref[0, rows[j], ls], b_ref[0, 0, rows[j], ls], lw_ref[0, 0, rows[j], ls]))
        preps = _rwkv_chunks(ins, cum, tri, lmask, eye, hm)
        outs = []
        for j in range(cu):
            for p in range(npair):
                rhat, ohat, g, hh = preps[j * npair + p]
                res = _rwkv_state_mm(jnp.concatenate([rhat, g], axis=0), sts[p], hm)
                outs.append(res[:CHUNK] + ohat)
                sts[p] = res[CHUNK:] + hh
        for j in range(cu):
            for p in range(npair):
                o_ref[0, rows[j], p * LANES:(p + 1) * LANES] = outs[j * npair + p]
        for p in range(npair):
            st_ref[p] = sts[p]
        return carry

    lax.fori_loop(0, nchunk // cu, body, 0)

    @pl.when(t == pl.num_programs(1) - 1)
    def _():
        sfin_ref[0] = st_ref[...]


def rwkv_scan(rev, d, r, v, kk, lw, kd, kb, s0, tb=512, cu=2):
    b, t, w = r.shape
    tb = min(tb, t)
    nt = t // tb
    nchunk = tb // CHUNK
    consts = _rwkv_consts(rev)

    def tmap(i):
        return (nt - 1 - i) if rev else i

    spec = pl.BlockSpec((1, tb, w), lambda bi, ti: (bi, tmap(ti), 0))
    spec2 = pl.BlockSpec((1, 1, tb, w), lambda bi, ti: (d, bi, tmap(ti), 0))
    st_spec = pl.BlockSpec((1,) + s0.shape[1:], lambda bi, ti: (bi, 0, 0, 0))
    return pl.pallas_call(
        functools.partial(_rwkv_kernel, rev=rev, nchunk=nchunk, cu=cu),
        grid=(b, nt),
        in_specs=[spec, spec, spec, spec2, spec2, spec2, st_spec] + [_full(c.shape) for c in consts],
        out_specs=[spec, st_spec],
        out_shape=[jax.ShapeDtypeStruct(r.shape, F32), jax.ShapeDtypeStruct(s0.shape, F32)],
        scratch_shapes=[pltpu.VMEM(s0.shape[1:], F32)],
        compiler_params=_params(("parallel", "arbitrary")),
    )(r, v, kk, lw, kd, kb, s0, *consts)


def _even_out_kernel(x_ref, g1_ref, of_ref, ob_ref, gh_ref, ng_ref, rf_ref, rb_ref, bonus_ref, gate_ref,
                     lng_ref, lnb_ref, ones_ref, w_ref, o_ref):
    z = of_ref[0] + ob_ref[0]
    gh = gh_ref[0]
    acc = None
    for h in range(HG_HEADS):
        ls = slice(h * HG_DV, (h + 1) * HG_DV)
        zh = z[:, ls]
        yh = zh * lax.rsqrt(jnp.mean(zh * zh, axis=-1, keepdims=True) + EPS) * ng_ref[...] * _silu(gh[:, ls])
        part = mm1(yh, w_ref[ls, :])
        acc = part if acc is None else acc + part
    o = rf_ref[0] + rb_ref[0]
    ones = ones_ref[...]
    mean = mmcr(o, ones) * (1.0 / RW_DH)
    cen = o - mean
    var = mmcr(cen * cen, ones) * (1.0 / RW_DH)
    o = cen * lax.rsqrt(var + RW_GN_EPS) * lng_ref[...] + lnb_ref[...]
    o = (o + bonus_ref[0]) * gate_ref[0]
    acc = acc + mm1(o, w_ref[HG_WIDTH:, :])
    o_ref[0] = x_ref[0] + g1_ref[0] * acc


def even_out(x, g1, o_f, o_b, p_hg, hg_g, rw_f, rw_b, bonus, gate, ln_g, ln_b, w_out, tm=256):
    b, t, d = x.shape
    tm = min(tm, t)
    tok = lambda w: pl.BlockSpec((1, tm, w), lambda bi, ti: (bi, ti, 0))
    hm = tok(RW_WIDTH)
    ones = np.kron(np.eye(RW_HEADS, dtype=np.float32), np.ones((RW_DH, RW_DH), np.float32))
    return pl.pallas_call(
        _even_out_kernel,
        grid=(b, t // tm),
        in_specs=[tok(d), pl.BlockSpec((1, 1, d), lambda bi, ti: (bi, 0, 0)), tok(HG_WIDTH), tok(HG_WIDTH),
                  pl.BlockSpec((1, tm, HG_WIDTH), lambda bi, ti: (bi, ti, 4)), _full((1, HG_DV)),
                  hm, hm, hm, hm, _full((1, RW_WIDTH)), _full((1, RW_WIDTH)), _full(ones.shape),
                  pl.BlockSpec(w_out.shape, lambda bi, ti: (0, 0), pipeline_mode=pl.Buffered(1))],
        out_specs=tok(d),
        out_shape=jax.ShapeDtypeStruct(x.shape, F32),
        compiler_params=_params(("parallel", "parallel")),
    )(x, g1.reshape(b, 1, d), o_f, o_b, p_hg, hg_g.reshape(1, HG_DV), rw_f, rw_b, bonus, gate,
      ln_g.reshape(1, RW_WIDTH), ln_b.reshape(1, RW_WIDTH), jnp.asarray(ones, BF16), w_out.astype(BF16))


def _cpow_table(zr, zi, n):
    def step(c, _):
        cr, ci = c
        return (cr * zr - ci * zi, cr * zi + ci * zr), (cr, ci)
    (_, _), (pr, pi) = lax.scan(step, (jnp.ones_like(zr), jnp.zeros_like(zr)), None, length=n)
    return pr, pi


def _s5_params(a_re, a_im, log_dt, b_re, b_im, c_re, c_im, rev, nsteps):
    ell = S5_CHUNK
    dt = jnp.exp(log_dt)[:, None]
    mag = jnp.exp(a_re * dt)
    lr, li = mag * jnp.cos(a_im * dt), mag * jnp.sin(a_im * dt)
    den = a_re * a_re + a_im * a_im
    fr = ((lr - 1.0) * a_re + li * a_im) / den
    fi = (li * a_re - (lr - 1.0) * a_im) / den
    bbr = fr[..., None] * b_re - fi[..., None] * b_im
    bbi = fr[..., None] * b_im + fi[..., None] * b_re
    pr, pi = _cpow_table(lr, li, ell + 1)
    cr_p = c_re[None] * pr[:, :, None, :] - c_im[None] * pi[:, :, None, :]
    ci_p = c_re[None] * pi[:, :, None, :] + c_im[None] * pr[:, :, None, :]
    kern = jnp.einsum('tgcn,gni->tgci', cr_p[:ell], bbr) - jnp.einsum('tgcn,gni->tgci', ci_p[:ell], bbi)
    s_idx = np.arange(ell)[:, None]
    t_idx = np.arange(ell)[None, :]
    lag = (s_idx - t_idx) if rev else (t_idx - s_idx)
    valid = jnp.asarray(lag >= 0)
    kt = kern[np.clip(lag, 0, ell - 1)]
    kt = jnp.where(valid[:, :, None, None, None], kt, 0.0)
    toep = jnp.transpose(kt, (2, 0, 4, 1, 3)).reshape(S5_GROUPS, S5_FEAT, S5_FEAT)
    e_idx = (np.arange(ell) if rev else (ell - 1 - np.arange(ell)))
    pwr, pwi = pr[e_idx], pi[e_idx]
    p_re = pwr[..., None] * bbr[None] - pwi[..., None] * bbi[None]
    p_im = pwr[..., None] * bbi[None] + pwi[..., None] * bbr[None]
    p_re = jnp.transpose(p_re, (1, 0, 3, 2)).reshape(S5_GROUPS, S5_FEAT, S5_STATE)
    p_im = jnp.transpose(p_im, (1, 0, 3, 2)).reshape(S5_GROUPS, S5_FEAT, S5_STATE)
    q_idx = ((ell - np.arange(ell)) if rev else (np.arange(ell) + 1))
    q_re = jnp.transpose(cr_p[q_idx], (1, 3, 0, 2)).reshape(S5_GROUPS, S5_STATE, S5_FEAT)
    q_im = -jnp.transpose(ci_p[q_idx], (1, 3, 0, 2)).reshape(S5_GROUPS, S5_STATE, S5_FEAT)
    zr, zi = [pr[ell]], [pi[ell]]
    for _ in range(nsteps - 1):
        zr, zi = zr + [zr[-1] * zr[-1] - zi[-1] * zi[-1]], zi + [2.0 * zr[-1] * zi[-1]]
    zr = jnp.stack(zr)[:, :, None, :]
    zi = jnp.stack(zi)[:, :, None, :]
    return toep, p_re, p_im, q_re, q_im, zr, zi


def _s5_expand(toep, p_re, p_im, q_re, q_im, zr, zi):
    ng = LANES // S5_GROUP
    nc = S5_GROUPS // ng
    ell, grp, st = S5_CHUNK, S5_GROUP, S5_STATE
    eye = jnp.eye(ng, dtype=BF16)
    kw = ell * LANES
    w = None if toep is None else jnp.einsum(
        'cgsitj,gh->csgithj', toep.astype(BF16).reshape(nc, ng, ell, grp, ell, grp), eye).reshape(nc, kw, kw)
    pe = lambda p: jnp.einsum('cgsin,gh->csgihn', p.astype(BF16).reshape(nc, ng, ell, grp, st), eye).reshape(nc, kw, ng * st)
    qe = lambda q: jnp.einsum('cgntj,gh->cgnthj', q.astype(BF16).reshape(nc, ng, st, ell, grp), eye).reshape(nc, ng * st, kw)
    p = jnp.concatenate([pe(p_re), pe(p_im)], axis=2)
    q = jnp.concatenate([qe(q_re), qe(q_im)], axis=1)
    ns = zr.shape[0]
    return w, p, q, zr.reshape(ns, nc, 1, ng * st), zi.reshape(ns, nc, 1, ng * st)


def _s5_kernel(*refs, rev, nsteps, intra):
    if intra:
        u_ref, w_ref, p_ref, q_ref, zr_ref, zi_ref, h0r_ref, h0i_ref, y_ref, hfr_ref, hfi_ref, cr_ref, ci_ref = refs
    else:
        u_ref, p_ref, q_ref, zr_ref, zi_ref, h0r_ref, h0i_ref, y_ref, hfr_ref, hfi_ref, cr_ref, ci_ref = refs
    t = pl.program_id(2)

    @pl.when(t == 0)
    def _():
        cr_ref[...] = h0r_ref[0, 0]
        ci_ref[...] = h0i_ref[0, 0]

    u = u_ref[0, 0].astype(BF16)
    jb = u.shape[0]
    half = p_ref.shape[-1] // 2
    row = lax.broadcasted_iota(jnp.int32, (jb, 1), 0)
    edge = (jb - 1) if rev else 0
    x = _dot(u, p_ref[0])
    xr, xi = x[:, :half], x[:, half:]
    car_r, car_i = cr_ref[...], ci_ref[...]
    z1r, z1i = zr_ref[0, 0], zi_ref[0, 0]
    at_edge = row == edge
    hr = xr + jnp.where(at_edge, z1r * car_r - z1i * car_i, 0.0)
    hi = xi + jnp.where(at_edge, z1r * car_i + z1i * car_r, 0.0)
    for s in range(nsteps):
        sh = 1 << s
        if sh >= jb:
            break
        zr, zi = zr_ref[s, 0], zi_ref[s, 0]
        if rev:
            pr_ = jnp.where(row < jb - sh, pltpu.roll(hr, jb - sh, axis=0), 0.0)
            pi_ = jnp.where(row < jb - sh, pltpu.roll(hi, jb - sh, axis=0), 0.0)
        else:
            pr_ = jnp.where(row >= sh, pltpu.roll(hr, sh, axis=0), 0.0)
            pi_ = jnp.where(row >= sh, pltpu.roll(hi, sh, axis=0), 0.0)
        hr, hi = hr + (zr * pr_ - zi * pi_), hi + (zr * pi_ + zi * pr_)
    if rev:
        hpr = jnp.where(at_edge, car_r, pltpu.roll(hr, jb - 1, axis=0))
        hpi = jnp.where(at_edge, car_i, pltpu.roll(hi, jb - 1, axis=0))
        cr_ref[...] = hr[0:1]
        ci_ref[...] = hi[0:1]
    else:
        hpr = jnp.where(at_edge, car_r, pltpu.roll(hr, 1, axis=0))
        hpi = jnp.where(at_edge, car_i, pltpu.roll(hi, 1, axis=0))
        cr_ref[...] = hr[jb - 1:jb]
        ci_ref[...] = hi[jb - 1:jb]
    y = _dot(jnp.concatenate([hpr, hpi], axis=1).astype(BF16), q_ref[0])
    if intra:
        y = y + _dot(u, w_ref[0])
    y_ref[0, 0] = y

    @pl.when(t == pl.num_programs(2) - 1)
    def _():
        hfr_ref[0, 0] = cr_ref[...]
        hfi_ref[0, 0] = ci_ref[...]


def s5_dir(rev, u4, w, p, q, zr, zi, h0r, h0i, jb=S5_ROWS):
    nc, nb, t, _ = u4.shape
    j = t // S5_CHUNK
    kw = S5_CHUNK * LANES
    uf = u4.reshape(nc, nb, j, kw)
    jb = min(jb, j)
    nj = j // jb
    nsteps = max(int(math.log2(jb)), 1)
    nst = h0r.shape[-1]

    def tmap(ti):
        return (nj - 1 - ti) if rev else ti

    const = lambda a: pl.BlockSpec((1,) + a.shape[1:], lambda ci, bi, ti: (ci, 0, 0), pipeline_mode=pl.Buffered(1))
    hspec = pl.BlockSpec((1, 1, 1, nst), lambda ci, bi, ti: (ci, bi, 0, 0))
    zspec = pl.BlockSpec((zr.shape[0], 1, 1, nst), lambda ci, bi, ti: (0, ci, 0, 0))
    uspec = pl.BlockSpec((1, 1, jb, kw), lambda ci, bi, ti: (ci, bi, tmap(ti), 0))
    mats = ([w] if w is not None else []) + [p, q]
    y, hfr, hfi = pl.pallas_call(
        functools.partial(_s5_kernel, rev=rev, nsteps=nsteps, intra=w is not None),
        grid=(nc, nb, nj),
        in_specs=[uspec] + [const(m) for m in mats] + [zspec, zspec, hspec, hspec],
        out_specs=[uspec, hspec, hspec],
        out_shape=[jax.ShapeDtypeStruct(uf.shape, F32), jax.ShapeDtypeStruct(h0r.shape, F32),
                   jax.ShapeDtypeStruct(h0i.shape, F32)],
        scratch_shapes=[pltpu.VMEM((1, nst), F32), pltpu.VMEM((1, nst), F32)],
        compiler_params=_params(("parallel", "parallel", "arbitrary")),
    )(uf, *mats, zr, zi, h0r, h0i)
    return y.reshape(u4.shape), hfr, hfi


def _gelu_tanh(x):
    return 0.5 * x * (1.0 + jnp.tanh(math.sqrt(2.0 / math.pi) * (x + 0.044715 * (x * x * x))))


def _odd_out_kernel(x_ref, g1_ref, of_ref, ob_ref, gg_ref, ng_ref, yf_ref, yb_ref, u_ref, dsk_ref,
                    wglu_ref, bglu_ref, w_ref, o_ref):
    z = of_ref[0] + ob_ref[0]
    gg = gg_ref[0]
    acc = None
    for h in range(GLA_HEADS):
        ls = slice(h * GLA_DV, (h + 1) * GLA_DV)
        zh = z[:, ls]
        yh = zh * lax.rsqrt(jnp.mean(zh * zh, axis=-1, keepdims=True) + EPS) * ng_ref[...] * _silu(gg[:, ls])
        part = mm1(yh, w_ref[ls, :])
        acc = part if acc is None else acc + part
    cat = lambda r: jnp.concatenate([r[c, 0] for c in range(r.shape[0])], axis=1)
    y = (cat(yf_ref) + cat(yb_ref)) + dsk_ref[...] * cat(u_ref)
    zz = _gelu_tanh(y)
    o_s5 = zz * jax.nn.sigmoid(mm1(zz, wglu_ref[...]) + bglu_ref[...])
    acc = acc + mm1(o_s5, w_ref[GLA_WIDTH:, :])
    o_ref[0] = x_ref[0] + g1_ref[0] * acc


def odd_out(x, g1, o_f, o_b, g_gla, gla_g, y_f, y_b, u, d_skip, w_glu, b_glu, w_out, tm=256):
    b, t, d = x.shape
    tm = min(tm, t)
    tok = lambda w: pl.BlockSpec((1, tm, w), lambda bi, ti: (bi, ti, 0))
    const = lambda a: pl.BlockSpec(a.shape, lambda bi, ti: (0, 0), pipeline_mode=pl.Buffered(1))
    blk = pl.BlockSpec((S5_WIDTH // LANES, 1, tm, LANES), lambda bi, ti: (0, bi, ti, 0))
    return pl.pallas_call(
        _odd_out_kernel,
        grid=(b, t // tm),
        in_specs=[tok(d), pl.BlockSpec((1, 1, d), lambda bi, ti: (bi, 0, 0)), tok(GLA_WIDTH), tok(GLA_WIDTH),
                  tok(GLA_WIDTH), _full((1, GLA_DV)), blk, blk, blk,
                  _full((1, S5_WIDTH)), const(w_glu), _full((1, S5_WIDTH)), const(w_out)],
        out_specs=tok(d),
        out_shape=jax.ShapeDtypeStruct(x.shape, F32),
        compiler_params=_params(("parallel", "parallel")),
    )(x, g1.reshape(b, 1, d), o_f, o_b, g_gla, gla_g.reshape(1, GLA_DV), y_f, y_b, u,
      d_skip.reshape(1, S5_WIDTH), w_glu.astype(BF16), b_glu.reshape(1, S5_WIDTH), w_out.astype(BF16))


def _ffn_kernel(x_ref, xp_ref, xn_ref, ng_ref, sc_ref, sh_ref, g2_ref, wup_ref, cw_ref, cb_ref, wdn_ref,
                fg_ref, o_ref, *, cols, fw, final):
    r = pl.program_id(1)
    nr = pl.num_programs(1)
    x = x_ref[0]
    n = x.shape[0]

    def modnorm(z):
        h = z * lax.rsqrt(jnp.mean(z * z, axis=-1, keepdims=True) + EPS) * ng_ref[...]
        return (h * (1.0 + sc_ref[0]) + sh_ref[0]).astype(BF16)

    h_main = modnorm(x)
    h_prev = jnp.where(r > 0, modnorm(xp_ref[0]), jnp.zeros((), BF16))
    h_next = jnp.where(r < nr - 1, modnorm(xn_ref[0]), jnp.zeros((), BF16))
    h_all = jnp.concatenate([h_prev, h_main, h_next], axis=0)
    pos = lax.broadcasted_iota(jnp.int32, (n, 1), 0)
    colid = jnp.bitwise_and(pos, cols - 1)
    first_col, last_col = colid == 0, colid == cols - 1
    slices = list(range(0, D_FF, fw))

    def up(f0):
        return (_dot(h_all, wup_ref[:, f0:f0 + fw]),
                _dot(h_main, wup_ref[:, D_FF + f0:D_FF + f0 + fw]))

    acc = None
    nxt = up(slices[0])
    for i, f0 in enumerate(slices):
        gate, val = nxt
        if i + 1 < len(slices):
            nxt = up(slices[i + 1])
        cw = lambda tap: cw_ref[tap:tap + 1, f0:f0 + fw]
        rows3 = [gate[di * cols:di * cols + n] for di in range(3)]
        left = rows3[0] * cw(0) + rows3[1] * cw(3) + rows3[2] * cw(6)
        mid = rows3[0] * cw(1) + rows3[1] * cw(4) + rows3[2] * cw(7)
        right = rows3[0] * cw(2) + rows3[1] * cw(5) + rows3[2] * cw(8)
        conv = (cb_ref[:, f0:f0 + fw] + mid
                + jnp.where(first_col, 0.0, pltpu.roll(left, 1, axis=0))
                + jnp.where(last_col, 0.0, pltpu.roll(right, n - 1, axis=0)))
        act = (_silu(conv) * val).astype(BF16)
        part = _dot(act, wdn_ref[f0:f0 + fw, :])
        acc = part if acc is None else acc + part
    y = x + g2_ref[0] * acc
    if final:
        y = y * lax.rsqrt(jnp.mean(y * y, axis=-1, keepdims=True) + EPS) * fg_ref[...]
    o_ref[0] = y


def conv_ffn(x, ng, sc, sh, g2, w_up, conv_w, conv_b, w_down, final_g, rows, cols, final, rt=16, fw=256):
    b, t, d = x.shape
    rt = min(rt, rows)
    nr = rows // rt
    n = rt * cols
    vec = lambda a: pl.BlockSpec((1, 1, d), lambda bi, ri: (bi, 0, 0))
    const = lambda a: pl.BlockSpec(a.shape, lambda bi, ri: (0, 0), pipeline_mode=pl.Buffered(1))
    wup = w_up.astype(BF16)
    wdn = w_down.astype(BF16)
    cw = conv_w.reshape(9, D_FF)
    cb = conv_b.reshape(1, D_FF)
    return pl.pallas_call(
        functools.partial(_ffn_kernel, cols=cols, fw=fw, final=final),
        grid=(b, nr),
        in_specs=[pl.BlockSpec((1, n, d), lambda bi, ri: (bi, ri, 0)),
                  pl.BlockSpec((1, cols, d), lambda bi, ri: (bi, jnp.maximum(ri * rt - 1, 0), 0)),
                  pl.BlockSpec((1, cols, d), lambda bi, ri: (bi, jnp.minimum((ri + 1) * rt, rows - 1), 0)),
                  _full((1, d)), vec(sc), vec(sh), vec(g2), const(wup), _full((9, D_FF)), _full((1, D_FF)),
                  const(wdn), _full((1, d))],
        out_specs=pl.BlockSpec((1, n, d), lambda bi, ri: (bi, ri, 0)),
        out_shape=jax.ShapeDtypeStruct(x.shape, F32),
        compiler_params=_params(("parallel", "parallel")),
    )(x, x, x, ng.reshape(1, d), sc.reshape(b, 1, d), sh.reshape(b, 1, d), g2.reshape(b, 1, d), wup, cw, cb,
      wdn, final_g.reshape(1, d))


def _even_layer(hx_args, ctx_args, prm, with_ctx_out):
    (w_in, w_out, lb, hg_g, mu, w0, w2, a0, a2, g2, k_k, k_a, r_k, ln_g, ln_b) = prm
    hg_cols = 5 * HG_WIDTH
    res = {}
    bsz = hx_args[0].shape[0]
    s_hg = [jnp.zeros((bsz, HG_HEADS, HG_DV, LANES), F32)] * 2
    s_rw = [jnp.zeros((bsz, RW_WIDTH // LANES, RW_DH, LANES), F32)] * 2
    for name, (x, g, sc, sh, g1), want_out in (("ctx", ctx_args, with_ctx_out), ("x", hx_args, True)):
        p_hg, p_rw = norm_proj(x, g, sc, sh, w_in, (hg_cols, RW_COLS))
        r, v, kk, gate, bonus, lw, kd, kb = rwkv_prep(p_rw, mu, w0, w2, a0, a2, g2, k_k, k_a, r_k)
        o_hg, o_rw = [], []
        for d in range(2):
            o, s_hg[d] = gated_scan("hgrn", d == 1, (p_hg, lb[d:d + 1], d), s_hg[d])
            o_hg.append(o)
            o, s_rw[d] = rwkv_scan(d == 1, d, r, v, kk, lw, kd, kb, s_rw[d])
            o_rw.append(o)
        if want_out:
            res[name] = even_out(x, g1, o_hg[0], o_hg[1], p_hg, hg_g, o_rw[0], o_rw[1], bonus, gate,
                                 ln_g, ln_b, w_out)
    return res


def _odd_layer(hx_args, ctx_args, prm, with_ctx_out):
    (w_in, w_out, gla_w2, gla_b, gla_g, a_re, a_im, log_dt, b_re, b_im, c_re, c_im, d_skip, w_glu, b_glu) = prm
    nq = 2 * GLA_QK + 2 * GLA_WIDTH
    w_perm = jnp.concatenate([w_in[:, :nq], w_in[:, nq + 2 * GLA_GATE_LORA:], w_in[:, nq:nq + 2 * GLA_GATE_LORA]], 1)
    z16 = jnp.zeros((GLA_GATE_LORA, GLA_QK), F32)
    w2pad = [jnp.concatenate([gla_w2[0], z16], 0).astype(BF16), jnp.concatenate([z16, gla_w2[1]], 0).astype(BF16)]
    res = {}
    bsz = hx_args[0].shape[0]
    s_gla = [jnp.zeros((bsz, GLA_HEADS, GLA_DV, LANES), F32)] * 2
    zs = jnp.zeros((S5_WIDTH // LANES, bsz, 1, (LANES // S5_GROUP) * S5_STATE), F32)
    s_s5 = [(zs, zs), (zs, zs)]
    s5_mats = []
    for d in range(2):
        toep, *rest = _s5_params(a_re[d], a_im[d], log_dt[d], b_re[d], b_im[d], c_re[d], c_im[d],
                                 rev=d == 1, nsteps=S5_MAX_STEPS)
        s5_mats.append((toep, *rest))
    s5_mats = [_s5_expand(s5_mats[0][0] + s5_mats[1][0], *s5_mats[0][1:]), _s5_expand(None, *s5_mats[1][1:])]
    for name, (x, g, sc, sh, g1), want_out in (("ctx", ctx_args, with_ctx_out), ("x", hx_args, True)):
        q, k, v, gg, u, gd = norm_proj(x, g, sc, sh, w_perm,
                                       (GLA_QK, GLA_QK, GLA_WIDTH, GLA_WIDTH, S5_WIDTH, 2 * GLA_GATE_LORA),
                                       blocked=(4,))
        o_gla, y_s5 = [], []
        for d in range(2):
            o, s_gla[d] = gated_scan("gla", d == 1, (q, k, v, gd, w2pad[d], gla_b[d].reshape(1, GLA_QK)), s_gla[d])
            o_gla.append(o)
            w_d, p_d, q_d, zr_d, zi_d = s5_mats[d]
            y, hr, hi = s5_dir(d == 1, u, w_d if d == 0 else None, p_d, q_d, zr_d, zi_d, s_s5[d][0], s_s5[d][1])
            s_s5[d] = (hr, hi)
            y_s5.append(y)
        if want_out:
            res[name] = odd_out(x, g1, o_gla[0], o_gla[1], gg, gla_g, y_s5[0], y_s5[1], u, d_skip, w_glu, b_glu,
                                w_out)
    return res


def kernel(x, c, ctx, c_ctx, ada_w, ada_b, norm1_g, norm2_g, final_g,
           ev_w_in, ev_w_out, hg_lb, hg_norm_g, rw_mu, rw_w0, rw_w2, rw_a0, rw_a2, rw_g2,
           rw_k_k, rw_k_a, rw_r_k, rw_ln_g, rw_ln_b,
           od_w_in, od_w_out, gla_w2, gla_b, gla_norm_g, s5_a_re, s5_a_im, s5_log_dt,
           s5_b_re, s5_b_im, s5_c_re, s5_c_im, s5_d, s5_w_glu, s5_b_glu,
           ffn_w_up, ffn_conv_w, ffn_conv_b, ffn_w_down):
    bsz, seq, d = x.shape
    depth = ada_w.shape[0]
    rows = seq // GRID_W
    ctx_len = ctx.shape[1]
    lb_all = jnp.cumsum(jax.nn.softmax(hg_lb.astype(F32), axis=0), axis=0)

    for layer in range(depth):
        last = layer == depth - 1
        j = layer // 2
        mod = jnp.split(jax.nn.silu(c) @ ada_w[layer] + ada_b[layer], 6, axis=-1)
        mod_c = jnp.split(jax.nn.silu(c_ctx) @ ada_w[layer] + ada_b[layer], 6, axis=-1)
        sh1, sc1, g1, sh2, sc2, g2 = mod
        csh1, csc1, cg1, csh2, csc2, cg2 = [jnp.broadcast_to(m[None], (bsz, d)) for m in mod_c]
        hx_args = (x, norm1_g[layer], sc1, sh1, g1)
        ctx_args = (ctx, norm1_g[layer], csc1, csh1, cg1)
        if layer % 2 == 0:
            prm = (ev_w_in[j], ev_w_out[j], lb_all[j], hg_norm_g[j], rw_mu[j], rw_w0[j], rw_w2[j], rw_a0[j],
                   rw_a2[j], rw_g2[j], rw_k_k[j], rw_k_a[j], rw_r_k[j], rw_ln_g[j], rw_ln_b[j])
            res = _even_layer(hx_args, ctx_args, prm, not last)
        else:
            prm = (od_w_in[j], od_w_out[j], gla_w2[j], gla_b[j], gla_norm_g[j], s5_a_re[j], s5_a_im[j],
                   s5_log_dt[j], s5_b_re[j], s5_b_im[j], s5_c_re[j], s5_c_im[j], s5_d[j], s5_w_glu[j], s5_b_glu[j])
            res = _odd_layer(hx_args, ctx_args, prm, not last)
        ffn = (ffn_w_up[layer], ffn_conv_w[layer], ffn_conv_b[layer], ffn_w_down[layer], final_g)
        x = conv_ffn(res["x"], norm2_g[layer], sc2, sh2, g2, *ffn, rows, GRID_W, final=last)
        if not last:
            ctx = conv_ffn(res["ctx"], norm2_g[layer], csc2, csh2, cg2, *ffn, 1, ctx_len, final=False)
    return x
```

```python
import functools
import math

import numpy as np
import jax
import jax.numpy as jnp
from jax import lax
from jax.experimental import pallas as pl
from jax.experimental.pallas import tpu as pltpu

F32 = jnp.float32
BF16 = jnp.bfloat16

EPS = 1e-6
GRID_W = 64
CHUNK = 64
LEVELS = 6

HG_HEADS, HG_DK, HG_DV = 4, 128, 128
HG_WIDTH = HG_HEADS * HG_DV
RW_HEADS, RW_DH = 8, 64
RW_WIDTH = RW_HEADS * RW_DH
RW_DECAY_LORA, RW_AAA_LORA, RW_GATE_LORA = 64, 64, 128
RW_GN_EPS = 64e-5
RW_COLS = 3 * RW_WIDTH + 2 * RW_DECAY_LORA + 2 * RW_AAA_LORA + RW_GATE_LORA
GLA_HEADS, GLA_DK, GLA_DV = 4, 64, 128
GLA_WIDTH = GLA_HEADS * GLA_DV
GLA_QK = GLA_HEADS * GLA_DK
GLA_GATE_LORA = 16
GLA_TAU = 16.0
S5_WIDTH, S5_GROUP, S5_STATE = 512, 16, 64
S5_GROUPS = S5_WIDTH // S5_GROUP
S5_CHUNK = 16
S5_FEAT = S5_CHUNK * S5_GROUP
S5_ROWS = 128
S5_MAX_STEPS = S5_ROWS.bit_length() - 1
D_FF = 2816

LANES = 128
VMEM_LIMIT = 56 * 1024 * 1024

NN = ((1,), (0,))
NT = ((1,), (1,))
TN = ((0,), (0,))


def _dot(a, b, dims=NN):
    return lax.dot_general(a, b, (dims, ((), ())), preferred_element_type=F32)


def _split(a):
    hi = a.astype(BF16)
    return hi, (a - hi.astype(F32)).astype(BF16)


def mm1(a, b, dims=NN):
    return _dot(a.astype(BF16), b.astype(BF16), dims)


def mm3(a, b, dims=NN):
    ah, al = _split(a)
    bh, bl = _split(b)
    return _dot(ah, bh, dims) + (_dot(al, bh, dims) + _dot(ah, bl, dims))


def mmc(c, x, dims=NN):
    xh, xl = _split(x)
    return _dot(c, xh, dims) + _dot(c, xl, dims)


def mmcr(x, c, dims=NN):
    xh, xl = _split(x)
    return _dot(xh, c, dims) + _dot(xl, c, dims)


def _silu(x):
    return x * jax.nn.sigmoid(x)


def _params(sem):
    return pltpu.CompilerParams(dimension_semantics=sem, vmem_limit_bytes=VMEM_LIMIT)


def _full(shape):
    nd = len(shape)
    return pl.BlockSpec(shape, lambda *_: (0,) * nd)


def _tau(rev):
    t = np.arange(CHUNK)
    return (CHUNK - 1 - t) if rev else t


def _level_masks(rev):
    tau = _tau(rev)
    ti, si = tau[:, None], tau[None, :]
    out = []
    for l in range(LEVELS):
        m = (((ti >> l) & 1) == 1) & (((si >> l) & 1) == 0) & ((ti >> (l + 1)) == (si >> (l + 1)))
        out.append(m)
    return np.stack(out).astype(np.float32)


def _gls_consts(rev):
    tau = _tau(rev)
    ti, ii = tau[:, None], tau[None, :]
    mats = [ii <= ti, ii > ti]
    sel = []
    for l in range(LEVELS):
        bit = (ti >> l) & 1
        start = (ti >> l) << l
        end = start + (1 << l) - 1
        mats.append(np.where(bit == 1, (ii >= start) & (ii <= ti), (ii > ti) & (ii <= end)))
        sel.append(np.broadcast_to(bit, (CHUNK, LANES)))
    mats.append(np.ones((8, CHUNK), bool))
    cstack = np.concatenate(mats, 0).astype(np.float32)
    return (jnp.asarray(cstack, BF16), jnp.asarray(np.stack(sel).astype(np.float32)),
            jnp.asarray(_level_masks(rev)), jnp.asarray(np.eye(CHUNK, dtype=np.float32)))


def _rwkv_consts(rev):
    tau = _tau(rev)
    ti, ii = tau[:, None], tau[None, :]
    cum = np.concatenate([ii <= ti, ii < ti, ii > ti, np.ones((8, CHUNK), bool)], 0).astype(np.float32)
    tri = np.stack([ii < ti, ii <= ti]).astype(np.float32)
    pair = lambda m: np.concatenate([m, m], axis=-1)
    hmask = np.zeros((2, LANES), np.float32)
    hmask[0, :RW_DH] = 1.0
    hmask[1, RW_DH:] = 1.0
    return (jnp.asarray(cum, BF16), jnp.asarray(pair(tri)), jnp.asarray(pair(_level_masks(rev))),
            jnp.asarray(pair(np.eye(CHUNK, dtype=np.float32))), jnp.asarray(hmask))


def _norm_proj_kernel(x_ref, g_ref, sc_ref, sh_ref, w_ref, *o_refs, splits, blocked):
    x = x_ref[...]
    h = x * lax.rsqrt(jnp.mean(x * x, axis=-1, keepdims=True) + EPS) * g_ref[...]
    h = (h * (1.0 + sc_ref[0]) + sh_ref[0]).astype(BF16)
    off = 0
    for i, (o_ref, n) in enumerate(zip(o_refs, splits)):
        res = _dot(h, w_ref[:, off:off + n])
        if i in blocked:
            for c in range(n // LANES):
                o_ref[c] = res[:, c * LANES:(c + 1) * LANES]
        else:
            o_ref[...] = res
        off += n


def norm_proj(x, g, sc, sh, w, splits, blocked=(), tm=256):
    b, t, d = x.shape
    tm = min(tm, t)
    steps = t // tm
    n = w.shape[1]
    out_specs, out_shape = [], []
    for i, s in enumerate(splits):
        if i in blocked:
            out_specs.append(pl.BlockSpec((s // LANES, tm, LANES), lambda i: (0, i, 0)))
            out_shape.append(jax.ShapeDtypeStruct((s // LANES, b * t, LANES), F32))
        else:
            out_specs.append(pl.BlockSpec((tm, s), lambda i: (i, 0)))
            out_shape.append(jax.ShapeDtypeStruct((b * t, s), F32))
    outs = pl.pallas_call(
        functools.partial(_norm_proj_kernel, splits=tuple(splits), blocked=tuple(blocked)),
        grid=(b * steps,),
        in_specs=[pl.BlockSpec((tm, d), lambda i: (i, 0)),
                  _full((1, d)),
                  pl.BlockSpec((1, 1, d), lambda i: (i // steps, 0, 0)),
                  pl.BlockSpec((1, 1, d), lambda i: (i // steps, 0, 0)),
                  pl.BlockSpec((d, n), lambda i: (0, 0), pipeline_mode=pl.Buffered(1))],
        out_specs=out_specs,
        out_shape=out_shape,
        compiler_params=_params(("parallel",)),
    )(x.reshape(b * t, d), g.reshape(1, d), sc.reshape(b, 1, d), sh.reshape(b, 1, d), w.astype(BF16))
    return [o.reshape(s // LANES, b, t, LANES) if i in blocked else o.reshape(b, t, s)
            for i, (o, s) in enumerate(zip(outs, splits))]


def _gls_chunks(groups, heads, cstack, sel, lmask, eye):
    ell = CHUNK
    c2 = jnp.concatenate([cstack, cstack], axis=1)
    es = [jnp.exp(_dot(c2, jnp.concatenate(_split(g), axis=0))) for _, _, g in groups]
    xs = [[e[(2 + l) * ell:(3 + l) * ell] * jnp.where(sel[l] > 0.0, q, k) for l in range(LEVELS)]
          for (q, k, _), e in zip(groups, es)]

    def msk(z, lm):
        return z if lm is None else z * lm

    att = [mm1(msk(groups[gi][0], lm), groups[gi][1], NT) * eye for gi, _, lm in heads]
    for l in range(LEVELS):
        att = [a + mm1(msk(xs[gi][l], lm), xs[gi][l], NT) * lmask[l] for a, (gi, _, lm) in zip(att, heads)]
    ov = [mm1(a, v) for a, (_, v, _) in zip(att, heads)]
    qb = [msk(groups[gi][0] * es[gi][0:ell], lm) for gi, _, lm in heads]
    kv = [mm1(v, msk(groups[gi][1] * es[gi][ell:2 * ell], lm), TN) for gi, v, lm in heads]
    dec = [e[8 * ell:8 * ell + 1] for e in es]
    return ov, qb, kv, dec


def _gls_kernel(*refs, mode, rev, nsub, nchunk, cu):
    if mode == "hgrn":
        (q_ref, v_ref, f_ref, lb_ref, s0_ref, cst_ref, sel_ref, lm_ref, eye_ref,
         o_ref, sfin_ref, st_ref) = refs
    else:
        (q_ref, k_ref, v_ref, gd_ref, w2_ref, gb_ref, hm_ref, s0_ref, cst_ref, sel_ref, lm_ref, eye_ref,
         o_ref, sfin_ref, st_ref) = refs
    t = pl.program_id(1)
    ngroups = q_ref.shape[-1] // LANES

    @pl.when(t == 0)
    def _():
        st_ref[...] = s0_ref[0]

    cstack = cst_ref[...]
    sel = [sel_ref[l] for l in range(LEVELS)]
    lmask = [lm_ref[l] for l in range(LEVELS)]
    eye = eye_ref[...]

    def body(it, carry):
        nh = ngroups * nsub
        state = [st_ref[h] for h in range(nh)]
        rows, loaded = [], []
        for j in range(cu):
            ci = it * cu + j
            c = (nchunk - 1 - ci) if rev else ci
            rows.append(pl.ds(pl.multiple_of(c * CHUNK, CHUNK), CHUNK))
            if mode == "hgrn":
                loaded.append((q_ref[0, rows[j], :], f_ref[0, rows[j], :], v_ref[0, rows[j], :]))
            else:
                loaded.append((q_ref[0, rows[j], :], k_ref[0, rows[j], :], gd_ref[0, rows[j], :],
                               v_ref[0, rows[j], :]))
        groups, heads = [], []
        for j in range(cu):
            if mode == "hgrn":
                qx, fx, v_all = loaded[j]
                q_all = _silu(qx) * (HG_DK ** -0.5)
                lb = lb_ref[...]
                f = lb + (1.0 - lb) * jax.nn.sigmoid(fx)
                k_all = 1.0 - f
                g_all = jnp.log(f)
            else:
                qx, k_all, gdx, v_all = loaded[j]
                q_all = qx * (GLA_DK ** -0.5)
                z = mm1(gdx, w2_ref[...]) + gb_ref[...]
                g_all = (jnp.minimum(z, 0.0) - jnp.log1p(jnp.exp(-jnp.abs(z)))) * (1.0 / GLA_TAU)
            for gi in range(ngroups):
                ls = slice(gi * LANES, (gi + 1) * LANES)
                groups.append((q_all[:, ls], k_all[:, ls], g_all[:, ls]))
                for i in range(nsub):
                    h = gi * nsub + i
                    heads.append((j * ngroups + gi, v_all[:, h * LANES:(h + 1) * LANES],
                                  None if nsub == 1 else hm_ref[i:i + 1, :]))
        ov, qb, kv, dec = _gls_chunks(groups, heads, cstack, sel, lmask, eye)
        results = []
        for idx, (gidx, _, lm) in enumerate(heads):
            j, h = idx // nh, idx % nh
            results.append((j, h, ov[idx] + mm1(qb[idx], state[h], NT)))
            st_new = state[h] * dec[gidx] + kv[idx]
            state[h] = st_new if lm is None else st_new * lm
        for j, h, o in results:
            o_ref[0, rows[j], h * LANES:(h + 1) * LANES] = o
        for h in range(nh):
            st_ref[h] = state[h]
        return carry

    lax.fori_loop(0, nchunk // cu, body, 0)

    @pl.when(t == pl.num_programs(1) - 1)
    def _():
        sfin_ref[0] = st_ref[...]


def gated_scan(mode, rev, arrays, s0, tb=512, cu=2):
    consts = _gls_consts(rev)
    if mode == "hgrn":
        p_hg, lb, d = arrays
        b, t, _ = p_hg.shape
        nh, nsub, width = HG_HEADS, 1, HG_WIDTH
    else:
        q, k, v, gd, w2pad, gb = arrays
        b, t, _ = q.shape
        nh, nsub, width = GLA_HEADS, 2, GLA_WIDTH
    tb = min(tb, t)
    nt = t // tb
    nchunk = tb // CHUNK

    def tmap(i):
        return (nt - 1 - i) if rev else i

    def col(c, w):
        return pl.BlockSpec((1, tb, w), lambda bi, ti: (bi, tmap(ti), c))

    st_spec = pl.BlockSpec((1, nh, s0.shape[2], LANES), lambda bi, ti: (bi, 0, 0, 0))
    cspecs = [_full(c.shape) for c in consts]
    if mode == "hgrn":
        ins = [p_hg, p_hg, p_hg, lb, s0, *consts]
        in_specs = [col(0, width), col(1, width), col(2 + d, width), _full((1, width)), st_spec, *cspecs]
    else:
        hm = np.zeros((2, LANES), np.float32)
        hm[0, :GLA_DK] = 1.0
        hm[1, GLA_DK:] = 1.0
        ins = [q, k, v, gd, w2pad, gb, jnp.asarray(hm), s0, *consts]
        in_specs = [col(0, GLA_QK), col(0, GLA_QK), col(0, width), col(0, 2 * GLA_GATE_LORA),
                    _full(w2pad.shape), _full(gb.shape), _full((2, LANES)), st_spec, *cspecs]
    o, sfin = pl.pallas_call(
        functools.partial(_gls_kernel, mode=mode, rev=rev, nsub=nsub, nchunk=nchunk, cu=cu),
        grid=(b, nt),
        in_specs=in_specs,
        out_specs=[pl.BlockSpec((1, tb, width), lambda bi, ti: (bi, tmap(ti), 0)), st_spec],
        out_shape=[jax.ShapeDtypeStruct((b, t, width), F32), jax.ShapeDtypeStruct(s0.shape, F32)],
        scratch_shapes=[pltpu.VMEM(s0.shape[1:], F32)],
        compiler_params=_params(("parallel", "arbitrary")),
    )(*ins)
    return o, sfin


def _rwkv_prep_kernel(p_ref, pp_ref, pn_ref, mu_ref, w0_ref, w2_ref, a0_ref, a2_ref, g2_ref, kk_ref, ka_ref,
                      rk_ref, ones_ref, r_o, v_o, kk_o, gate_o, bonus_o, lw_o, kd_o, kb_o):
    t = pl.program_id(1)
    nt = pl.num_programs(1)
    p = p_ref[0]
    tb = p.shape[0]
    row = lax.broadcasted_iota(jnp.int32, (tb, 1), 0)
    prev_row = jnp.where(t > 0, pp_ref[0, 7:8, :], 0.0)
    next_row = jnp.where(t < nt - 1, pn_ref[0, 0:1, :], 0.0)
    prev = jnp.where(row == 0, prev_row, pltpu.roll(p, 1, axis=0))
    nxt = jnp.where(row == tb - 1, next_row, pltpu.roll(p, tb - 1, axis=0))
    s = p + mu_ref[0:1, :] * (prev - p) + mu_ref[1:2, :] * (nxt - p)
    w = RW_WIDTH
    r, k, v = s[:, 0:w], s[:, w:2 * w], s[:, 2 * w:3 * w]
    wd = jnp.tanh(s[:, 3 * w:3 * w + LANES])
    ad = s[:, 3 * w + LANES:3 * w + 2 * LANES]
    gd = s[:, 3 * w + 2 * LANES:3 * w + 3 * LANES]
    ones = ones_ref[...]
    kk = k * kk_ref[...]
    nrm = jnp.sqrt(mmcr(kk * kk, ones))
    kk = kk / jnp.maximum(nrm, 1e-12)
    gate = mm1(jax.nn.sigmoid(gd), g2_ref[...])
    ksum = jnp.zeros_like(k)
    for d in range(2):
        zw = w0_ref[d:d + 1, :] + mm1(wd, w2_ref[d])
        w_log = -(jnp.maximum(-zw, 0.0) + jnp.log1p(jnp.exp(-jnp.abs(zw)))) - 0.5
        lw = -jnp.exp(w_log)
        a = jax.nn.sigmoid(a0_ref[d:d + 1, :] + mm1(ad, a2_ref[d]))
        kd = k * (1.0 + (a - 1.0) * ka_ref[...])
        ksum = ksum + kd
        lw_o[d, 0] = lw
        kd_o[d, 0] = kd
        kb_o[d, 0] = kk * a
    r_o[0] = r
    v_o[0] = v
    kk_o[0] = kk
    gate_o[0] = gate
    bonus_o[0] = mmcr(r * ksum * rk_ref[...], ones) * v


def rwkv_prep(p_rw, mu, w0, w2, a0, a2, g2, k_k, k_a, r_k, tb=256):
    b, t, c = p_rw.shape
    tb = min(tb, t)
    nt = t // tb
    w = RW_WIDTH
    z = jnp.zeros((RW_DECAY_LORA, w), F32)
    w2p = jnp.stack([jnp.concatenate([w2[0], z], 0), jnp.concatenate([z, w2[1]], 0)]).astype(BF16)
    a2p = jnp.stack([jnp.concatenate([a2[0], z], 0), jnp.concatenate([z, a2[1]], 0)]).astype(BF16)
    ones = np.kron(np.eye(RW_HEADS, dtype=np.float32), np.ones((RW_DH, RW_DH), np.float32))
    nb8 = t // 8
    hm = jax.ShapeDtypeStruct((b, t, w), F32)
    hm2 = jax.ShapeDtypeStruct((2, b, t, w), F32)
    hm_spec = pl.BlockSpec((1, tb, w), lambda bi, ti: (bi, ti, 0))
    hm2_spec = pl.BlockSpec((2, 1, tb, w), lambda bi, ti: (0, bi, ti, 0))
    return pl.pallas_call(
        _rwkv_prep_kernel,
        grid=(b, nt),
        in_specs=[pl.BlockSpec((1, tb, c), lambda bi, ti: (bi, ti, 0)),
                  pl.BlockSpec((1, 8, c), lambda bi, ti: (bi, jnp.maximum(ti * (tb // 8) - 1, 0), 0)),
                  pl.BlockSpec((1, 8, c), lambda bi, ti: (bi, jnp.minimum((ti + 1) * (tb // 8), nb8 - 1), 0)),
                  _full((2, c)), _full((2, w)), _full((2, LANES, w)), _full((2, w)), _full((2, LANES, w)),
                  _full((RW_GATE_LORA, w)), _full((1, w)), _full((1, w)), _full((1, w)), _full((w, w))],
        out_specs=[hm_spec] * 5 + [hm2_spec] * 3,
        out_shape=[hm] * 5 + [hm2] * 3,
        compiler_params=_params(("parallel", "parallel")),
    )(p_rw, p_rw, p_rw, mu, w0, w2p, a0, a2p, g2.astype(BF16), k_k.reshape(1, w), k_a.reshape(1, w),
      r_k.reshape(1, w), jnp.asarray(ones, BF16))


def _bd(y, hm):
    return jnp.concatenate([y * hm[0], y * hm[1]], axis=0).astype(BF16)


def _bdiag(z, hm):
    return z[:CHUNK] * hm[0] + z[CHUNK:] * hm[1]


def _rwkv_chunks(ins, cum, tri, lmask, eye, hm):
    ell = CHUNK
    nc = range(len(ins))
    cum2 = jnp.concatenate([cum, cum], axis=1)
    cs = [_dot(cum2, jnp.concatenate(_split(x[5]), axis=0)) for x in ins]
    rt = [ins[i][0] * jnp.exp(cs[i][0:ell]) for i in nc]
    at = [ins[i][3] * jnp.exp(cs[i][ell:2 * ell]) for i in nc]
    e_neg = [jnp.exp(-cs[i][0:ell]) for i in nc]
    e_rem = [jnp.exp(cs[i][2 * ell:3 * ell]) for i in nc]
    e_tot = [jnp.exp(cs[i][3 * ell:3 * ell + 1]) for i in nc]
    ar = [jnp.concatenate([at[i], rt[i]], axis=0).astype(BF16) for i in nc]
    pb = [_dot(ar[i], _bd(ins[i][4] * e_neg[i], hm), NT) for i in nc]
    pk = [_dot(ar[i], _bd(ins[i][1] * e_neg[i], hm), NT) for i in nc]
    n = [pb[i][:ell] * tri[0] for i in nc]
    ti = [eye + n[i] * lmask[0] for i in nc]
    for l in range(1, LEVELS):
        p = [_dot((n[i] * lmask[l]).astype(BF16), _bd(ti[i], hm)) for i in nc]
        ti = [ti[i] + _dot(ti[i].astype(BF16), _bd(p[i], hm)) for i in nc]
    vbd = [_bd(ins[i][2], hm) for i in nc]
    mv = [_dot((pk[i][:ell] * tri[0]).astype(BF16), vbd[i]) for i in nc]
    au = [_dot(ti[i].astype(BF16), jnp.concatenate([_bd(at[i], hm), _bd(mv[i], hm)], axis=1)) for i in nc]
    ro = [_dot((pb[i][ell:] * tri[1]).astype(BF16),
               jnp.concatenate([_bd(au[i][:, :LANES], hm), _bd(au[i][:, LANES:], hm)], axis=1)) for i in nc]
    rk = [_dot((pk[i][ell:] * tri[1]).astype(BF16), vbd[i]) for i in nc]
    gh = [_dot((ins[i][4] * e_rem[i]).astype(BF16), au[i].astype(BF16), TN) for i in nc]
    kv = [_dot((ins[i][1] * e_rem[i]).astype(BF16), ins[i][2].astype(BF16), TN) for i in nc]
    out = []
    for i in nc:
        rhat = rt[i] + ro[i][:, :LANES]
        ohat = ro[i][:, LANES:] + rk[i]
        g = eye * e_tot[i] + _bdiag(gh[i][:, :LANES], hm)
        h = _bdiag(gh[i][:, LANES:], hm) + _bdiag(kv[i], hm)
        out.append((rhat, ohat, g, h))
    return out


def _rwkv_state_mm(x, st, hm):
    s_hi = st.astype(BF16).astype(F32)
    bdh, bdl = _bd(s_hi, hm), _bd(st - s_hi, hm)
    xh, xl = _split(x)
    return _dot(jnp.concatenate([xh, xl], axis=1), jnp.concatenate([bdh, bdh], axis=0)) + _dot(xh, bdl)


def _rwkv_kernel(r_ref, v_ref, a_ref, lw_ref, k_ref, b_ref, s0_ref, cum_ref, tri_ref, lm_ref, eye_ref, hm_ref,
                 o_ref, sfin_ref, st_ref, *, rev, nchunk, cu):
    t = pl.program_id(1)
    npair = r_ref.shape[-1] // LANES

    @pl.when(t == 0)
    def _():
        st_ref[...] = s0_ref[0]

    cum = cum_ref[...]
    tri = [tri_ref[0], tri_ref[1]]
    lmask = [lm_ref[l] for l in range(LEVELS)]
    eye = eye_ref[...]
    hm = [hm_ref[0:1, :], hm_ref[1:2, :]]

    def body(it, carry):
        sts = [st_ref[p] for p in range(npair)]
        rows, ins = [], []
        for j in range(cu):
            ci = it * cu + j
            c = (nchunk - 1 - ci) if rev else ci
            rows.append(pl.ds(pl.multiple_of(c * CHUNK, CHUNK), CHUNK))
            for p in range(npair):
                ls = slice(p * LANES, (p + 1) * LANES)
                ins.append((r_ref[0, rows[j], ls], k_ref[0, 0, rows[j], ls], v_ref[0, rows[j], ls],
                            -a_ref[0, rows[j], ls], b_ref[0, 0, rows[j], ls], lw_ref[0, 0, rows[j], ls]))
        preps = _rwkv_chunks(ins, cum, tri, lmask, eye, hm)
        outs = []
        for j in range(cu):
            for p in range(npair):
                rhat, ohat, g, hh = preps[j * npair + p]
                res = _rwkv_state_mm(jnp.concatenate([rhat, g], axis=0), sts[p], hm)
                outs.append(res[:CHUNK] + ohat)
                sts[p] = res[CHUNK:] + hh
        for j in range(cu):
            for p in range(npair):
                o_ref[0, rows[j], p * LANES:(p + 1) * LANES] = outs[j * npair + p]
        for p in range(npair):
            st_ref[p] = sts[p]
        return carry

    lax.fori_loop(0, nchunk // cu, body, 0)

    @pl.when(t == pl.num_programs(1) - 1)
    def _():
        sfin_ref[0] = st_ref[...]


def rwkv_scan(rev, d, r, v, kk, lw, kd, kb, s0, tb=512, cu=2):
    b, t, w = r.shape
    tb = min(tb, t)
    nt = t // tb
    nchunk = tb // CHUNK
    consts = _rwkv_consts(rev)

    def tmap(i):
        return (nt - 1 - i) if rev else i

    spec = pl.BlockSpec((1, tb, w), lambda bi, ti: (bi, tmap(ti), 0))
    spec2 = pl.BlockSpec((1, 1, tb, w), lambda bi, ti: (d, bi, tmap(ti), 0))
    st_spec = pl.BlockSpec((1,) + s0.shape[1:], lambda bi, ti: (bi, 0, 0, 0))
    return pl.pallas_call(
        functools.partial(_rwkv_kernel, rev=rev, nchunk=nchunk, cu=cu),
        grid=(b, nt),
        in_specs=[spec, spec, spec, spec2, spec2, spec2, st_spec] + [_full(c.shape) for c in consts],
        out_specs=[spec, st_spec],
        out_shape=[jax.ShapeDtypeStruct(r.shape, F32), jax.ShapeDtypeStruct(s0.shape, F32)],
        scratch_shapes=[pltpu.VMEM(s0.shape[1:], F32)],
        compiler_params=_params(("parallel", "arbitrary")),
    )(r, v, kk, lw, kd, kb, s0, *consts)


def _even_out_kernel(x_ref, g1_ref, of_ref, ob_ref, gh_ref, ng_ref, rf_ref, rb_ref, bonus_ref, gate_ref,
                     lng_ref, lnb_ref, ones_ref, w_ref, o_ref):
    z = of_ref[0] + ob_ref[0]
    gh = gh_ref[0]
    acc = None
    for h in range(HG_HEADS):
        ls = slice(h * HG_DV, (h + 1) * HG_DV)
        zh = z[:, ls]
        yh = zh * lax.rsqrt(jnp.mean(zh * zh, axis=-1, keepdims=True) + EPS) * ng_ref[...] * _silu(gh[:, ls])
        part = mm1(yh, w_ref[ls, :])
        acc = part if acc is None else acc + part
    o = rf_ref[0] + rb_ref[0]
    ones = ones_ref[...]
    mean = mmcr(o, ones) * (1.0 / RW_DH)
    cen = o - mean
    var = mmcr(cen * cen, ones) * (1.0 / RW_DH)
    o = cen * lax.rsqrt(var + RW_GN_EPS) * lng_ref[...] + lnb_ref[...]
    o = (o + bonus_ref[0]) * gate_ref[0]
    acc = acc + mm1(o, w_ref[HG_WIDTH:, :])
    o_ref[0] = x_ref[0] + g1_ref[0] * acc


def even_out(x, g1, o_f, o_b, p_hg, hg_g, rw_f, rw_b, bonus, gate, ln_g, ln_b, w_out, tm=256):
    b, t, d = x.shape
    tm = min(tm, t)
    tok = lambda w: pl.BlockSpec((1, tm, w), lambda bi, ti: (bi, ti, 0))
    hm = tok(RW_WIDTH)
    ones = np.kron(np.eye(RW_HEADS, dtype=np.float32), np.ones((RW_DH, RW_DH), np.float32))
    return pl.pallas_call(
        _even_out_kernel,
        grid=(b, t // tm),
        in_specs=[tok(d), pl.BlockSpec((1, 1, d), lambda bi, ti: (bi, 0, 0)), tok(HG_WIDTH), tok(HG_WIDTH),
                  pl.BlockSpec((1, tm, HG_WIDTH), lambda bi, ti: (bi, ti, 4)), _full((1, HG_DV)),
                  hm, hm, hm, hm, _full((1, RW_WIDTH)), _full((1, RW_WIDTH)), _full(ones.shape),
                  pl.BlockSpec(w_out.shape, lambda bi, ti: (0, 0), pipeline_mode=pl.Buffered(1))],
        out_specs=tok(d),
        out_shape=jax.ShapeDtypeStruct(x.shape, F32),
        compiler_params=_params(("parallel", "parallel")),
    )(x, g1.reshape(b, 1, d), o_f, o_b, p_hg, hg_g.reshape(1, HG_DV), rw_f, rw_b, bonus, gate,
      ln_g.reshape(1, RW_WIDTH), ln_b.reshape(1, RW_WIDTH), jnp.asarray(ones, BF16), w_out.astype(BF16))


def _cpow_table(zr, zi, n):
    def step(c, _):
        cr, ci = c
        return (cr * zr - ci * zi, cr * zi + ci * zr), (cr, ci)
    (_, _), (pr, pi) = lax.scan(step, (jnp.ones_like(zr), jnp.zeros_like(zr)), None, length=n)
    return pr, pi


def _s5_params(a_re, a_im, log_dt, b_re, b_im, c_re, c_im, rev, nsteps):
    ell = S5_CHUNK
    dt = jnp.exp(log_dt)[:, None]
    mag = jnp.exp(a_re * dt)
    lr, li = mag * jnp.cos(a_im * dt), mag * jnp.sin(a_im * dt)
    den = a_re * a_re + a_im * a_im
    fr = ((lr - 1.0) * a_re + li * a_im) / den
    fi = (li * a_re - (lr - 1.0) * a_im) / den
    bbr = fr[..., None] * b_re - fi[..., None] * b_im
    bbi = fr[..., None] * b_im + fi[..., None] * b_re
    pr, pi = _cpow_table(lr, li, ell + 1)
    pr_t, pi_t = jnp.transpose(pr, (1, 2, 0)), jnp.transpose(pi, (1, 2, 0))
    cre_t, cim_t = jnp.swapaxes(c_re, 1, 2), jnp.swapaxes(c_im, 1, 2)
    cr_p = cre_t[:, :, None, :] * pr_t[..., None] - cim_t[:, :, None, :] * pi_t[..., None]
    ci_p = cre_t[:, :, None, :] * pi_t[..., None] + cim_t[:, :, None, :] * pr_t[..., None]
    kern = (jnp.einsum('gntc,gni->gitc', cr_p[:, :, :ell], bbr)
            - jnp.einsum('gntc,gni->gitc', ci_p[:, :, :ell], bbi))
    s_idx = np.arange(ell)[:, None]
    t_idx = np.arange(ell)[None, :]
    lag = (s_idx - t_idx) if rev else (t_idx - s_idx)
    valid = jnp.asarray(lag >= 0)
    kt = jnp.take(kern, np.clip(lag, 0, ell - 1).reshape(-1), axis=2)
    kt = jnp.where(valid[None, None, :, :, None], kt.reshape(S5_GROUPS, S5_GROUP, ell, ell, S5_GROUP), 0.0)
    toep = jnp.transpose(kt, (0, 2, 1, 3, 4)).reshape(S5_GROUPS, S5_FEAT, S5_FEAT)
    e_idx = (np.arange(ell) if rev else (ell - 1 - np.arange(ell)))
    pwr, pwi = jnp.swapaxes(pr[e_idx], 0, 1), jnp.swapaxes(pi[e_idx], 0, 1)
    bbr_t, bbi_t = jnp.swapaxes(bbr, 1, 2), jnp.swapaxes(bbi, 1, 2)
    p_re = pwr[:, :, None, :] * bbr_t[:, None] - pwi[:, :, None, :] * bbi_t[:, None]
    p_im = pwr[:, :, None, :] * bbi_t[:, None] + pwi[:, :, None, :] * bbr_t[:, None]
    p_re = p_re.reshape(S5_GROUPS, S5_FEAT, S5_STATE)
    p_im = p_im.reshape(S5_GROUPS, S5_FEAT, S5_STATE)
    q_sl = slice(ell, 0, -1) if rev else slice(1, ell + 1)
    q_re = cr_p[:, :, q_sl].reshape(S5_GROUPS, S5_STATE, S5_FEAT)
    q_im = -ci_p[:, :, q_sl].reshape(S5_GROUPS, S5_STATE, S5_FEAT)
    zr, zi = [pr[ell]], [pi[ell]]
    for _ in range(nsteps - 1):
        zr, zi = zr + [zr[-1] * zr[-1] - zi[-1] * zi[-1]], zi + [2.0 * zr[-1] * zi[-1]]
    zr = jnp.stack(zr)[:, :, None, :]
    zi = jnp.stack(zi)[:, :, None, :]
    return toep, p_re, p_im, q_re, q_im, zr, zi


def _s5_expand(toep, p_re, p_im, q_re, q_im, zr, zi):
    ng = LANES // S5_GROUP
    nc = S5_GROUPS // ng
    ell, grp, st = S5_CHUNK, S5_GROUP, S5_STATE
    eye = jnp.eye(ng, dtype=BF16)
    kw = ell * LANES
    w = None if toep is None else (
        toep.astype(BF16).reshape(nc, ng, ell * grp, ell, 1, grp) * eye[None, :, None, None, :, None]
    ).reshape(nc, kw, kw)
    pe = lambda p: (p.astype(BF16).reshape(nc, ng, ell * grp, 1, st) * eye[None, :, None, :, None]
                    ).reshape(nc, kw, ng * st)
    qe = lambda q: (q.astype(BF16).reshape(nc, ng, st, ell, 1, grp) * eye[None, :, None, None, :, None]
                    ).reshape(nc, ng * st, kw)
    p = jnp.concatenate([pe(p_re), pe(p_im)], axis=2)
    q = jnp.concatenate([qe(q_re), qe(q_im)], axis=1)
    ns = zr.shape[0]
    return w, p, q, zr.reshape(ns, nc, 1, ng * st), zi.reshape(ns, nc, 1, ng * st)


def _s5_kernel(*refs, rev, nsteps, intra):
    if intra:
        u_ref, w_ref, p_ref, q_ref, zr_ref, zi_ref, h0r_ref, h0i_ref, y_ref, hfr_ref, hfi_ref, cr_ref, ci_ref = refs
    else:
        u_ref, p_ref, q_ref, zr_ref, zi_ref, h0r_ref, h0i_ref, y_ref, hfr_ref, hfi_ref, cr_ref, ci_ref = refs
    t = pl.program_id(2)

    @pl.when(t == 0)
    def _():
        cr_ref[...] = h0r_ref[0, 0]
        ci_ref[...] = h0i_ref[0, 0]

    ng = LANES // S5_GROUP
    jb = u_ref.shape[2] // S5_CHUNK
    pieces = [u_ref[0, 0, pl.ds(s, jb, stride=S5_CHUNK), :] for s in range(S5_CHUNK)]
    slot = lax.broadcasted_iota(jnp.int32, (1, LANES), 1) // S5_GROUP
    cols = []
    for g in range(ng):
        for h in range(S5_CHUNK // ng):
            acc = None
            for sp in range(ng):
                shift = ((sp - g) * S5_GROUP) % LANES
                piece = pieces[h * ng + sp]
                rolled = piece if shift == 0 else pltpu.roll(piece, shift, axis=1)
                acc = rolled if acc is None else jnp.where(slot == sp, rolled, acc)
            cols.append(acc.astype(BF16))
    u = jnp.concatenate(cols, axis=1)
    half = p_ref.shape[-1] // 2
    row = lax.broadcasted_iota(jnp.int32, (jb, 1), 0)
    edge = (jb - 1) if rev else 0
    x = _dot(u, p_ref[0])
    xr, xi = x[:, :half], x[:, half:]
    car_r, car_i = cr_ref[...], ci_ref[...]
    z1r, z1i = zr_ref[0, 0], zi_ref[0, 0]
    at_edge = row == edge
    hr = xr + jnp.where(at_edge, z1r * car_r - z1i * car_i, 0.0)
    hi = xi + jnp.where(at_edge, z1r * car_i + z1i * car_r, 0.0)
    for s in range(nsteps):
        sh = 1 << s
        if sh >= jb:
            break
        zr, zi = zr_ref[s, 0], zi_ref[s, 0]
        if rev:
            pr_ = jnp.where(row < jb - sh, pltpu.roll(hr, jb - sh, axis=0), 0.0)
            pi_ = jnp.where(row < jb - sh, pltpu.roll(hi, jb - sh, axis=0), 0.0)
        else:
            pr_ = jnp.where(row >= sh, pltpu.roll(hr, sh, axis=0), 0.0)
            pi_ = jnp.where(row >= sh, pltpu.roll(hi, sh, axis=0), 0.0)
        hr, hi = hr + (zr * pr_ - zi * pi_), hi + (zr * pi_ + zi * pr_)
    if rev:
        hpr = jnp.where(at_edge, car_r, pltpu.roll(hr, jb - 1, axis=0))
        hpi = jnp.where(at_edge, car_i, pltpu.roll(hi, jb - 1, axis=0))
        cr_ref[...] = hr[0:1]
        ci_ref[...] = hi[0:1]
    else:
        hpr = jnp.where(at_edge, car_r, pltpu.roll(hr, 1, axis=0))
        hpi = jnp.where(at_edge, car_i, pltpu.roll(hi, 1, axis=0))
        cr_ref[...] = hr[jb - 1:jb]
        ci_ref[...] = hi[jb - 1:jb]
    y = _dot(jnp.concatenate([hpr, hpi], axis=1).astype(BF16), q_ref[0])
    if intra:
        y = y + _dot(u, w_ref[0])
    for s in range(S5_CHUNK):
        y_ref[0, 0, pl.ds(s, jb, stride=S5_CHUNK), :] = y[:, s * LANES:(s + 1) * LANES]

    @pl.when(t == pl.num_programs(2) - 1)
    def _():
        hfr_ref[0, 0] = cr_ref[...]
        hfi_ref[0, 0] = ci_ref[...]


def s5_dir(rev, u4, w, p, q, zr, zi, h0r, h0i, jb=S5_ROWS):
    nc, nb, t, _ = u4.shape
    j = t // S5_CHUNK
    jb = min(jb, j)
    nj = j // jb
    nsteps = max(int(math.log2(jb)), 1)
    nst = h0r.shape[-1]

    def tmap(ti):
        return (nj - 1 - ti) if rev else ti

    const = lambda a: pl.BlockSpec((1,) + a.shape[1:], lambda ci, bi, ti: (ci, 0, 0), pipeline_mode=pl.Buffered(1))
    hspec = pl.BlockSpec((1, 1, 1, nst), lambda ci, bi, ti: (ci, bi, 0, 0))
    zspec = pl.BlockSpec((zr.shape[0], 1, 1, nst), lambda ci, bi, ti: (0, ci, 0, 0))
    uspec = pl.BlockSpec((1, 1, jb * S5_CHUNK, LANES), lambda ci, bi, ti: (ci, bi, tmap(ti), 0))
    mats = ([w] if w is not None else []) + [p, q]
    y, hfr, hfi = pl.pallas_call(
        functools.partial(_s5_kernel, rev=rev, nsteps=nsteps, intra=w is not None),
        grid=(nc, nb, nj),
        in_specs=[uspec] + [const(m) for m in mats] + [zspec, zspec, hspec, hspec],
        out_specs=[uspec, hspec, hspec],
        out_shape=[jax.ShapeDtypeStruct(u4.shape, F32), jax.ShapeDtypeStruct(h0r.shape, F32),
                   jax.ShapeDtypeStruct(h0i.shape, F32)],
        scratch_shapes=[pltpu.VMEM((1, nst), F32), pltpu.VMEM((1, nst), F32)],
        compiler_params=_params(("parallel", "parallel", "arbitrary")),
    )(u4, *mats, zr, zi, h0r, h0i)
    return y, hfr, hfi


def _gelu_tanh(x):
    return 0.5 * x * (1.0 + jnp.tanh(math.sqrt(2.0 / math.pi) * (x + 0.044715 * (x * x * x))))


def _odd_out_kernel(x_ref, g1_ref, of_ref, ob_ref, gg_ref, ng_ref, yf_ref, yb_ref, u_ref, dsk_ref,
                    wglu_ref, bglu_ref, w_ref, o_ref):
    z = of_ref[0] + ob_ref[0]
    gg = gg_ref[0]
    acc = None
    for h in range(GLA_HEADS):
        ls = slice(h * GLA_DV, (h + 1) * GLA_DV)
        zh = z[:, ls]
        yh = zh * lax.rsqrt(jnp.mean(zh * zh, axis=-1, keepdims=True) + EPS) * ng_ref[...] * _silu(gg[:, ls])
        part = mm1(yh, w_ref[ls, :])
        acc = part if acc is None else acc + part
    cat = lambda r: jnp.concatenate([r[c, 0] for c in range(r.shape[0])], axis=1)
    y = (cat(yf_ref) + cat(yb_ref)) + dsk_ref[...] * cat(u_ref)
    zz = _gelu_tanh(y)
    o_s5 = zz * jax.nn.sigmoid(mm1(zz, wglu_ref[...]) + bglu_ref[...])
    acc = acc + mm1(o_s5, w_ref[GLA_WIDTH:, :])
    o_ref[0] = x_ref[0] + g1_ref[0] * acc


def odd_out(x, g1, o_f, o_b, g_gla, gla_g, y_f, y_b, u, d_skip, w_glu, b_glu, w_out, tm=256):
    b, t, d = x.shape
    tm = min(tm, t)
    tok = lambda w: pl.BlockSpec((1, tm, w), lambda bi, ti: (bi, ti, 0))
    const = lambda a: pl.BlockSpec(a.shape, lambda bi, ti: (0, 0), pipeline_mode=pl.Buffered(1))
    blk = pl.BlockSpec((S5_WIDTH // LANES, 1, tm, LANES), lambda bi, ti: (0, bi, ti, 0))
    return pl.pallas_call(
        _odd_out_kernel,
        grid=(b, t // tm),
        in_specs=[tok(d), pl.BlockSpec((1, 1, d), lambda bi, ti: (bi, 0, 0)), tok(GLA_WIDTH), tok(GLA_WIDTH),
                  tok(GLA_WIDTH), _full((1, GLA_DV)), blk, blk, blk,
                  _full((1, S5_WIDTH)), const(w_glu), _full((1, S5_WIDTH)), const(w_out)],
        out_specs=tok(d),
        out_shape=jax.ShapeDtypeStruct(x.shape, F32),
        compiler_params=_params(("parallel", "parallel")),
    )(x, g1.reshape(b, 1, d), o_f, o_b, g_gla, gla_g.reshape(1, GLA_DV), y_f, y_b, u,
      d_skip.reshape(1, S5_WIDTH), w_glu.astype(BF16), b_glu.reshape(1, S5_WIDTH), w_out.astype(BF16))


def _ffn_kernel(x_ref, xp_ref, xn_ref, ng_ref, sc_ref, sh_ref, g2_ref, wup_ref, cw_ref, cb_ref, wdn_ref,
                fg_ref, o_ref, *, cols, fw, ahead, final):
    r = pl.program_id(1)
    nr = pl.num_programs(1)
    x = x_ref[0]
    n = x.shape[0]

    def modnorm(z):
        h = z * lax.rsqrt(jnp.mean(z * z, axis=-1, keepdims=True) + EPS) * ng_ref[...]
        return (h * (1.0 + sc_ref[0]) + sh_ref[0]).astype(BF16)

    h_main = modnorm(x)
    h_prev = jnp.where(r > 0, modnorm(xp_ref[0]), jnp.zeros((), BF16))
    h_next = jnp.where(r < nr - 1, modnorm(xn_ref[0]), jnp.zeros((), BF16))
    h_all = jnp.concatenate([h_prev, h_main, h_next], axis=0)
    pos = lax.broadcasted_iota(jnp.int32, (n, 1), 0)
    colid = jnp.bitwise_and(pos, cols - 1)
    first_col, last_col = colid == 0, colid == cols - 1
    slices = list(range(0, D_FF, fw))

    def up(f0):
        return (_dot(h_all, wup_ref[:, f0:f0 + fw]),
                _dot(h_main, wup_ref[:, D_FF + f0:D_FF + f0 + fw]))

    acc = None
    ups = [up(f0) for f0 in slices[:ahead]]
    for i, f0 in enumerate(slices):
        gate, val = ups[i]
        if i + ahead < len(slices):
            ups.append(up(slices[i + ahead]))
        cw = lambda tap: cw_ref[tap:tap + 1, f0:f0 + fw]
        rows3 = [gate[di * cols:di * cols + n] for di in range(3)]
        left = rows3[0] * cw(0) + rows3[1] * cw(3) + rows3[2] * cw(6)
        mid = rows3[0] * cw(1) + rows3[1] * cw(4) + rows3[2] * cw(7)
        right = rows3[0] * cw(2) + rows3[1] * cw(5) + rows3[2] * cw(8)
        conv = (cb_ref[:, f0:f0 + fw] + mid
                + jnp.where(first_col, 0.0, pltpu.roll(left, 1, axis=0))
                + jnp.where(last_col, 0.0, pltpu.roll(right, n - 1, axis=0)))
        act = (_silu(conv) * val).astype(BF16)
        part = _dot(act, wdn_ref[f0:f0 + fw, :])
        acc = part if acc is None else acc + part
    y = x + g2_ref[0] * acc
    if final:
        y = y * lax.rsqrt(jnp.mean(y * y, axis=-1, keepdims=True) + EPS) * fg_ref[...]
    o_ref[0] = y


def conv_ffn(x, ng, sc, sh, g2, w_up, conv_w, conv_b, w_down, final_g, rows, cols, final, rt=16, fw=256,
             ahead=3):
    b, t, d = x.shape
    rt = min(rt, rows)
    nr = rows // rt
    n = rt * cols
    vec = lambda a: pl.BlockSpec((1, 1, d), lambda bi, ri: (bi, 0, 0))
    const = lambda a: pl.BlockSpec(a.shape, lambda bi, ri: (0, 0), pipeline_mode=pl.Buffered(1))
    wup = w_up.astype(BF16)
    wdn = w_down.astype(BF16)
    cw = conv_w.reshape(9, D_FF)
    cb = conv_b.reshape(1, D_FF)
    return pl.pallas_call(
        functools.partial(_ffn_kernel, cols=cols, fw=fw, ahead=ahead, final=final),
        grid=(b, nr),
        in_specs=[pl.BlockSpec((1, n, d), lambda bi, ri: (bi, ri, 0)),
                  pl.BlockSpec((1, cols, d), lambda bi, ri: (bi, jnp.maximum(ri * rt - 1, 0), 0)),
                  pl.BlockSpec((1, cols, d), lambda bi, ri: (bi, jnp.minimum((ri + 1) * rt, rows - 1), 0)),
                  _full((1, d)), vec(sc), vec(sh), vec(g2), const(wup), _full((9, D_FF)), _full((1, D_FF)),
                  const(wdn), _full((1, d))],
        out_specs=pl.BlockSpec((1, n, d), lambda bi, ri: (bi, ri, 0)),
        out_shape=jax.ShapeDtypeStruct(x.shape, F32),
        compiler_params=_params(("parallel", "parallel")),
    )(x, x, x, ng.reshape(1, d), sc.reshape(b, 1, d), sh.reshape(b, 1, d), g2.reshape(b, 1, d), wup, cw, cb,
      wdn, final_g.reshape(1, d))


def _even_layer(hx_args, ctx_args, prm, with_ctx_out):
    (w_in, w_out, lb, hg_g, mu, w0, w2, a0, a2, g2, k_k, k_a, r_k, ln_g, ln_b) = prm
    hg_cols = 5 * HG_WIDTH
    res = {}
    bsz = hx_args[0].shape[0]
    s_hg = [jnp.zeros((bsz, HG_HEADS, HG_DV, LANES), F32)] * 2
    s_rw = [jnp.zeros((bsz, RW_WIDTH // LANES, RW_DH, LANES), F32)] * 2
    for name, (x, g, sc, sh, g1), want_out in (("ctx", ctx_args, with_ctx_out), ("x", hx_args, True)):
        p_hg, p_rw = norm_proj(x, g, sc, sh, w_in, (hg_cols, RW_COLS))
        r, v, kk, gate, bonus, lw, kd, kb = rwkv_prep(p_rw, mu, w0, w2, a0, a2, g2, k_k, k_a, r_k)
        o_hg, o_rw = [], []
        for d in range(2):
            o, s_hg[d] = gated_scan("hgrn", d == 1, (p_hg, lb[d:d + 1], d), s_hg[d])
            o_hg.append(o)
            o, s_rw[d] = rwkv_scan(d == 1, d, r, v, kk, lw, kd, kb, s_rw[d])
            o_rw.append(o)
        if want_out:
            res[name] = even_out(x, g1, o_hg[0], o_hg[1], p_hg, hg_g, o_rw[0], o_rw[1], bonus, gate,
                                 ln_g, ln_b, w_out)
    return res


def _odd_layer(hx_args, ctx_args, prm, with_ctx_out):
    (w_in, w_out, gla_w2, gla_b, gla_g, a_re, a_im, log_dt, b_re, b_im, c_re, c_im, d_skip, w_glu, b_glu) = prm
    nq = 2 * GLA_QK + 2 * GLA_WIDTH
    w_perm = jnp.concatenate([w_in[:, :nq], w_in[:, nq + 2 * GLA_GATE_LORA:], w_in[:, nq:nq + 2 * GLA_GATE_LORA]], 1)
    z16 = jnp.zeros((GLA_GATE_LORA, GLA_QK), F32)
    w2pad = [jnp.concatenate([gla_w2[0], z16], 0).astype(BF16), jnp.concatenate([z16, gla_w2[1]], 0).astype(BF16)]
    res = {}
    bsz = hx_args[0].shape[0]
    s_gla = [jnp.zeros((bsz, GLA_HEADS, GLA_DV, LANES), F32)] * 2
    zs = jnp.zeros((S5_WIDTH // LANES, bsz, 1, (LANES // S5_GROUP) * S5_STATE), F32)
    s_s5 = [(zs, zs), (zs, zs)]
    s5_mats = []
    for d in range(2):
        toep, *rest = _s5_params(a_re[d], a_im[d], log_dt[d], b_re[d], b_im[d], c_re[d], c_im[d],
                                 rev=d == 1, nsteps=S5_MAX_STEPS)
        s5_mats.append((toep, *rest))
    s5_mats = [_s5_expand(s5_mats[0][0] + s5_mats[1][0], *s5_mats[0][1:]), _s5_expand(None, *s5_mats[1][1:])]
    for name, (x, g, sc, sh, g1), want_out in (("ctx", ctx_args, with_ctx_out), ("x", hx_args, True)):
        q, k, v, gg, u, gd = norm_proj(x, g, sc, sh, w_perm,
                                       (GLA_QK, GLA_QK, GLA_WIDTH, GLA_WIDTH, S5_WIDTH, 2 * GLA_GATE_LORA),
                                       blocked=(4,))
        o_gla, y_s5 = [], []
        for d in range(2):
            o, s_gla[d] = gated_scan("gla", d == 1, (q, k, v, gd, w2pad[d], gla_b[d].reshape(1, GLA_QK)), s_gla[d])
            o_gla.append(o)
            w_d, p_d, q_d, zr_d, zi_d = s5_mats[d]
            y, hr, hi = s5_dir(d == 1, u, w_d if d == 0 else None, p_d, q_d, zr_d, zi_d, s_s5[d][0], s_s5[d][1])
            s_s5[d] = (hr, hi)
            y_s5.append(y)
        if want_out:
            res[name] = odd_out(x, g1, o_gla[0], o_gla[1], gg, gla_g, y_s5[0], y_s5[1], u, d_skip, w_glu, b_glu,
                                w_out)
    return res


def kernel(x, c, ctx, c_ctx, ada_w, ada_b, norm1_g, norm2_g, final_g,
           ev_w_in, ev_w_out, hg_lb, hg_norm_g, rw_mu, rw_w0, rw_w2, rw_a0, rw_a2, rw_g2,
           rw_k_k, rw_k_a, rw_r_k, rw_ln_g, rw_ln_b,
           od_w_in, od_w_out, gla_w2, gla_b, gla_norm_g, s5_a_re, s5_a_im, s5_log_dt,
           s5_b_re, s5_b_im, s5_c_re, s5_c_im, s5_d, s5_w_glu, s5_b_glu,
           ffn_w_up, ffn_conv_w, ffn_conv_b, ffn_w_down):
    bsz, seq, d = x.shape
    depth = ada_w.shape[0]
    rows = seq // GRID_W
    ctx_len = ctx.shape[1]
    lb_all = jnp.cumsum(jax.nn.softmax(hg_lb.astype(F32), axis=0), axis=0)

    for layer in range(depth):
        last = layer == depth - 1
        j = layer // 2
        mod = jnp.split(jax.nn.silu(c) @ ada_w[layer] + ada_b[layer], 6, axis=-1)
        mod_c = jnp.split(jax.nn.silu(c_ctx) @ ada_w[layer] + ada_b[layer], 6, axis=-1)
        sh1, sc1, g1, sh2, sc2, g2 = mod
        csh1, csc1, cg1, csh2, csc2, cg2 = [jnp.broadcast_to(m[None], (bsz, d)) for m in mod_c]
        hx_args = (x, norm1_g[layer], sc1, sh1, g1)
        ctx_args = (ctx, norm1_g[layer], csc1, csh1, cg1)
        if layer % 2 == 0:
            prm = (ev_w_in[j], ev_w_out[j], lb_all[j], hg_norm_g[j], rw_mu[j], rw_w0[j], rw_w2[j], rw_a0[j],
                   rw_a2[j], rw_g2[j], rw_k_k[j], rw_k_a[j], rw_r_k[j], rw_ln_g[j], rw_ln_b[j])
            res = _even_layer(hx_args, ctx_args, prm, not last)
        else:
            prm = (od_w_in[j], od_w_out[j], gla_w2[j], gla_b[j], gla_norm_g[j], s5_a_re[j], s5_a_im[j],
                   s5_log_dt[j], s5_b_re[j], s5_b_im[j], s5_c_re[j], s5_c_im[j], s5_d[j], s5_w_glu[j], s5_b_glu[j])
            res = _odd_layer(hx_args, ctx_args, prm, not last)
        ffn = (ffn_w_up[layer], ffn_conv_w[layer], ffn_conv_b[layer], ffn_w_down[layer], final_g)
        x = conv_ffn(res["x"], norm2_g[layer], sc2, sh2, g2, *ffn, rows, GRID_W, final=last)
        if not last:
            ctx = conv_ffn(res["ctx"], norm2_g[layer], csc2, csh2, cg2, *ffn, 1, ctx_len, final=False)
    return x
```

```python
import functools
import math

import numpy as np
import jax
import jax.numpy as jnp
from jax import lax
from jax.experimental import pallas as pl
from jax.experimental.pallas import tpu as pltpu

F32 = jnp.float32
BF16 = jnp.bfloat16

EPS = 1e-6
GRID_W = 64
CHUNK = 64
LEVELS = 6

HG_HEADS, HG_DK, HG_DV = 4, 128, 128
HG_WIDTH = HG_HEADS * HG_DV
RW_HEADS, RW_DH = 8, 64
RW_WIDTH = RW_HEADS * RW_DH
RW_DECAY_LORA, RW_AAA_LORA, RW_GATE_LORA = 64, 64, 128
RW_GN_EPS = 64e-5
RW_COLS = 3 * RW_WIDTH + 2 * RW_DECAY_LORA + 2 * RW_AAA_LORA + RW_GATE_LORA
GLA_HEADS, GLA_DK, GLA_DV = 4, 64, 128
GLA_WIDTH = GLA_HEADS * GLA_DV
GLA_QK = GLA_HEADS * GLA_DK
GLA_GATE_LORA = 16
GLA_TAU = 16.0
S5_WIDTH, S5_GROUP, S5_STATE = 512, 16, 64
S5_GROUPS = S5_WIDTH // S5_GROUP
S5_CHUNK = 16
S5_FEAT = S5_CHUNK * S5_GROUP
S5_ROWS = 256
S5_MAX_STEPS = S5_ROWS.bit_length() - 1
D_FF = 2816

LANES = 128
VMEM_LIMIT = 56 * 1024 * 1024

NN = ((1,), (0,))
NT = ((1,), (1,))
TN = ((0,), (0,))


def _dot(a, b, dims=NN):
    return lax.dot_general(a, b, (dims, ((), ())), preferred_element_type=F32)


def _split(a):
    hi = a.astype(BF16)
    return hi, (a - hi.astype(F32)).astype(BF16)


def mm1(a, b, dims=NN):
    return _dot(a.astype(BF16), b.astype(BF16), dims)


def mm3(a, b, dims=NN):
    ah, al = _split(a)
    bh, bl = _split(b)
    return _dot(ah, bh, dims) + (_dot(al, bh, dims) + _dot(ah, bl, dims))


def mmc(c, x, dims=NN):
    xh, xl = _split(x)
    return _dot(c, xh, dims) + _dot(c, xl, dims)


def mmcr(x, c, dims=NN):
    xh, xl = _split(x)
    return _dot(xh, c, dims) + _dot(xl, c, dims)


def _silu(x):
    return x * jax.nn.sigmoid(x)


def _params(sem):
    return pltpu.CompilerParams(dimension_semantics=sem, vmem_limit_bytes=VMEM_LIMIT)


def _full(shape):
    nd = len(shape)
    return pl.BlockSpec(shape, lambda *_: (0,) * nd)


def _tau(rev):
    t = np.arange(CHUNK)
    return (CHUNK - 1 - t) if rev else t


def _level_masks(rev):
    tau = _tau(rev)
    ti, si = tau[:, None], tau[None, :]
    out = []
    for l in range(LEVELS):
        m = (((ti >> l) & 1) == 1) & (((si >> l) & 1) == 0) & ((ti >> (l + 1)) == (si >> (l + 1)))
        out.append(m)
    return np.stack(out).astype(np.float32)


def _gls_consts(rev):
    tau = _tau(rev)
    ti, ii = tau[:, None], tau[None, :]
    mats = [ii <= ti, ii > ti]
    sel = []
    for l in range(LEVELS):
        bit = (ti >> l) & 1
        start = (ti >> l) << l
        end = start + (1 << l) - 1
        mats.append(np.where(bit == 1, (ii >= start) & (ii <= ti), (ii > ti) & (ii <= end)))
        sel.append(np.broadcast_to(bit, (CHUNK, LANES)))
    mats.append(np.ones((8, CHUNK), bool))
    cstack = np.concatenate(mats, 0).astype(np.float32)
    return (jnp.asarray(cstack, BF16), jnp.asarray(np.stack(sel).astype(np.float32)),
            jnp.asarray(_level_masks(rev)), jnp.asarray(np.eye(CHUNK, dtype=np.float32)))


def _rwkv_consts(rev):
    tau = _tau(rev)
    ti, ii = tau[:, None], tau[None, :]
    cum = np.concatenate([ii <= ti, ii < ti, ii > ti, np.ones((8, CHUNK), bool)], 0).astype(np.float32)
    tri = np.stack([ii < ti, ii <= ti]).astype(np.float32)
    pair = lambda m: np.concatenate([m, m], axis=-1)
    hmask = np.zeros((2, LANES), np.float32)
    hmask[0, :RW_DH] = 1.0
    hmask[1, RW_DH:] = 1.0
    return (jnp.asarray(cum, BF16), jnp.asarray(pair(tri)), jnp.asarray(pair(_level_masks(rev))),
            jnp.asarray(pair(np.eye(CHUNK, dtype=np.float32))), jnp.asarray(hmask))


def _norm_proj_kernel(x_ref, g_ref, sc_ref, sh_ref, w_ref, *o_refs, splits, blocked):
    x = x_ref[...]
    h = x * lax.rsqrt(jnp.mean(x * x, axis=-1, keepdims=True) + EPS) * g_ref[...]
    h = (h * (1.0 + sc_ref[0]) + sh_ref[0]).astype(BF16)
    off = 0
    for i, (o_ref, n) in enumerate(zip(o_refs, splits)):
        res = _dot(h, w_ref[:, off:off + n])
        if i in blocked:
            for c in range(n // LANES):
                o_ref[c] = res[:, c * LANES:(c + 1) * LANES]
        else:
            o_ref[...] = res
        off += n


def norm_proj(x, g, sc, sh, w, splits, blocked=(), tm=512):
    b, t, d = x.shape
    tm = min(tm, t)
    steps = t // tm
    n = w.shape[1]
    out_specs, out_shape = [], []
    for i, s in enumerate(splits):
        if i in blocked:
            out_specs.append(pl.BlockSpec((s // LANES, tm, LANES), lambda i: (0, i, 0)))
            out_shape.append(jax.ShapeDtypeStruct((s // LANES, b * t, LANES), F32))
        else:
            out_specs.append(pl.BlockSpec((tm, s), lambda i: (i, 0)))
            out_shape.append(jax.ShapeDtypeStruct((b * t, s), F32))
    outs = pl.pallas_call(
        functools.partial(_norm_proj_kernel, splits=tuple(splits), blocked=tuple(blocked)),
        grid=(b * steps,),
        in_specs=[pl.BlockSpec((tm, d), lambda i: (i, 0)),
                  _full((1, d)),
                  pl.BlockSpec((1, 1, d), lambda i: (i // steps, 0, 0)),
                  pl.BlockSpec((1, 1, d), lambda i: (i // steps, 0, 0)),
                  pl.BlockSpec((d, n), lambda i: (0, 0), pipeline_mode=pl.Buffered(1))],
        out_specs=out_specs,
        out_shape=out_shape,
        compiler_params=_params(("parallel",)),
    )(x.reshape(b * t, d), g.reshape(1, d), sc.reshape(b, 1, d), sh.reshape(b, 1, d), w.astype(BF16))
    return [o.reshape(s // LANES, b, t, LANES) if i in blocked else o.reshape(b, t, s)
            for i, (o, s) in enumerate(zip(outs, splits))]


def _gls_chunks(groups, heads, cstack, sel, lmask, eye):
    ell = CHUNK
    c2 = jnp.concatenate([cstack, cstack], axis=1)
    es = []
    for i in range(0, len(groups), 2):
        rhs = jnp.concatenate([jnp.concatenate(_split(g), axis=0) for _, _, g in groups[i:i + 2]], axis=1)
        e2 = jnp.exp(_dot(c2, rhs))
        es += [e2[:, :LANES], e2[:, LANES:]]
    xs = [[e[(2 + l) * ell:(3 + l) * ell] * jnp.where(sel[l] > 0.0, q, k) for l in range(LEVELS)]
          for (q, k, _), e in zip(groups, es)]

    def msk(z, lm):
        return z if lm is None else z * lm

    if heads[0][2] is None:
        att = [mm1(groups[gi][0], groups[gi][1], NT) * eye for gi, _, _ in heads]
        for l in range(LEVELS):
            att = [a + mm1(xs[gi][l], xs[gi][l], NT) * lmask[l] for a, (gi, _, _) in zip(att, heads)]
        ov = [mm1(a, v) for a, (_, v, _) in zip(att, heads)]
    else:
        pairs = [(heads[i], heads[i + 1]) for i in range(0, len(heads), 2)]
        hm = [pairs[0][0][2], pairs[0][1][2]]
        two = lambda m: jnp.concatenate([m, m], axis=1)
        att = [_dot(groups[a[0]][0].astype(BF16), _bd(groups[a[0]][1], hm), NT) * two(eye) for a, _ in pairs]
        for l in range(LEVELS):
            lm2 = two(lmask[l])
            att = [t + _dot(xs[a[0]][l].astype(BF16), _bd(xs[a[0]][l], hm), NT) * lm2
                   for t, (a, _) in zip(att, pairs)]
        zv = jnp.zeros((ell, LANES), BF16)
        ov = []
        for t, (a, b) in zip(att, pairs):
            va, vb = a[1].astype(BF16), b[1].astype(BF16)
            vbd = jnp.concatenate([jnp.concatenate([va, zv], axis=1), jnp.concatenate([zv, vb], axis=1)], axis=0)
            o2 = _dot(t.astype(BF16), vbd)
            ov += [o2[:, :LANES], o2[:, LANES:]]
    qb = [msk(groups[gi][0] * es[gi][0:ell], lm) for gi, _, lm in heads]
    kv = [mm1(v, msk(groups[gi][1] * es[gi][ell:2 * ell], lm), TN) for gi, v, lm in heads]
    dec = [e[8 * ell:8 * ell + 1] for e in es]
    return ov, qb, kv, dec


def _gls_kernel(*refs, mode, rev, nsub, nchunk, cu):
    if mode == "hgrn":
        (q_ref, v_ref, f_ref, lb_ref, s0_ref, cst_ref, sel_ref, lm_ref, eye_ref,
         o_ref, sfin_ref, st_ref) = refs
    else:
        (q_ref, k_ref, v_ref, gd_ref, w2_ref, gb_ref, hm_ref, s0_ref, cst_ref, sel_ref, lm_ref, eye_ref,
         o_ref, sfin_ref, st_ref) = refs
    t = pl.program_id(1)
    ngroups = q_ref.shape[-1] // LANES

    @pl.when(t == 0)
    def _():
        st_ref[...] = s0_ref[0]

    cstack = cst_ref[...]
    sel = [sel_ref[l] for l in range(LEVELS)]
    lmask = [lm_ref[l] for l in range(LEVELS)]
    eye = eye_ref[...]

    def body(it, carry):
        nh = ngroups * nsub
        state = [st_ref[h] for h in range(nh)]
        rows, loaded = [], []
        for j in range(cu):
            ci = it * cu + j
            c = (nchunk - 1 - ci) if rev else ci
            rows.append(pl.ds(pl.multiple_of(c * CHUNK, CHUNK), CHUNK))
            if mode == "hgrn":
                loaded.append((q_ref[0, rows[j], :], f_ref[0, rows[j], :], v_ref[0, rows[j], :]))
            else:
                loaded.append((q_ref[0, rows[j], :], k_ref[0, rows[j], :], gd_ref[0, rows[j], :],
                               v_ref[0, rows[j], :]))
        groups, heads = [], []
        for j in range(cu):
            if mode == "hgrn":
                qx, fx, v_all = loaded[j]
                q_all = _silu(qx) * (HG_DK ** -0.5)
                lb = lb_ref[...]
                f = lb + (1.0 - lb) * jax.nn.sigmoid(fx)
                k_all = 1.0 - f
                g_all = jnp.log(f)
            else:
                qx, k_all, gdx, v_all = loaded[j]
                q_all = qx * (GLA_DK ** -0.5)
                z = mm1(gdx, w2_ref[...]) + gb_ref[...]
                g_all = (jnp.minimum(z, 0.0) - jnp.log1p(jnp.exp(-jnp.abs(z)))) * (1.0 / GLA_TAU)
            for gi in range(ngroups):
                ls = slice(gi * LANES, (gi + 1) * LANES)
                groups.append((q_all[:, ls], k_all[:, ls], g_all[:, ls]))
                for i in range(nsub):
                    h = gi * nsub + i
                    heads.append((j * ngroups + gi, v_all[:, h * LANES:(h + 1) * LANES],
                                  None if nsub == 1 else hm_ref[i:i + 1, :]))
        ov, qb, kv, dec = _gls_chunks(groups, heads, cstack, sel, lmask, eye)
        results = []
        for idx, (gidx, _, lm) in enumerate(heads):
            j, h = idx // nh, idx % nh
            results.append((j, h, ov[idx] + mm1(qb[idx], state[h], NT)))
            st_new = state[h] * dec[gidx] + kv[idx]
            state[h] = st_new if lm is None else st_new * lm
        for j, h, o in results:
            o_ref[0, rows[j], h * LANES:(h + 1) * LANES] = o
        for h in range(nh):
            st_ref[h] = state[h]
        return carry

    lax.fori_loop(0, nchunk // cu, body, 0)

    @pl.when(t == pl.num_programs(1) - 1)
    def _():
        sfin_ref[0] = st_ref[...]


def gated_scan(mode, rev, arrays, s0, tb=512, cu=4):
    consts = _gls_consts(rev)
    if mode == "hgrn":
        p_hg, lb, d = arrays
        b, t, _ = p_hg.shape
        nh, nsub, width = HG_HEADS, 1, HG_WIDTH
    else:
        q, k, v, gd, w2pad, gb = arrays
        b, t, _ = q.shape
        nh, nsub, width = GLA_HEADS, 2, GLA_WIDTH
    tb = min(tb, t)
    nt = t // tb
    nchunk = tb // CHUNK
    cu = min(cu, nchunk)

    def tmap(i):
        return (nt - 1 - i) if rev else i

    def col(c, w):
        return pl.BlockSpec((1, tb, w), lambda bi, ti: (bi, tmap(ti), c))

    st_spec = pl.BlockSpec((1, nh, s0.shape[2], LANES), lambda bi, ti: (bi, 0, 0, 0))
    cspecs = [_full(c.shape) for c in consts]
    if mode == "hgrn":
        ins = [p_hg, p_hg, p_hg, lb, s0, *consts]
        in_specs = [col(0, width), col(1, width), col(2 + d, width), _full((1, width)), st_spec, *cspecs]
    else:
        hm = np.zeros((2, LANES), np.float32)
        hm[0, :GLA_DK] = 1.0
        hm[1, GLA_DK:] = 1.0
        ins = [q, k, v, gd, w2pad, gb, jnp.asarray(hm), s0, *consts]
        in_specs = [col(0, GLA_QK), col(0, GLA_QK), col(0, width), col(0, 2 * GLA_GATE_LORA),
                    _full(w2pad.shape), _full(gb.shape), _full((2, LANES)), st_spec, *cspecs]
    o, sfin = pl.pallas_call(
        functools.partial(_gls_kernel, mode=mode, rev=rev, nsub=nsub, nchunk=nchunk, cu=cu),
        grid=(b, nt),
        in_specs=in_specs,
        out_specs=[pl.BlockSpec((1, tb, width), lambda bi, ti: (bi, tmap(ti), 0)), st_spec],
        out_shape=[jax.ShapeDtypeStruct((b, t, width), F32), jax.ShapeDtypeStruct(s0.shape, F32)],
        scratch_shapes=[pltpu.VMEM(s0.shape[1:], F32)],
        compiler_params=_params(("parallel", "arbitrary")),
    )(*ins)
    return o, sfin


def _rwkv_prep_kernel(p_ref, pp_ref, pn_ref, mu_ref, w0_ref, w2_ref, a0_ref, a2_ref, g2_ref, kk_ref, ka_ref,
                      rk_ref, ones_ref, r_o, v_o, kk_o, gate_o, bonus_o, lw_o, kd_o, kb_o):
    t = pl.program_id(1)
    nt = pl.num_programs(1)
    p = p_ref[0]
    tb = p.shape[0]
    row = lax.broadcasted_iota(jnp.int32, (tb, 1), 0)
    prev_row = jnp.where(t > 0, pp_ref[0, 7:8, :], 0.0)
    next_row = jnp.where(t < nt - 1, pn_ref[0, 0:1, :], 0.0)
    prev = jnp.where(row == 0, prev_row, pltpu.roll(p, 1, axis=0))
    nxt = jnp.where(row == tb - 1, next_row, pltpu.roll(p, tb - 1, axis=0))
    s = p + mu_ref[0:1, :] * (prev - p) + mu_ref[1:2, :] * (nxt - p)
    w = RW_WIDTH
    r, k, v = s[:, 0:w], s[:, w:2 * w], s[:, 2 * w:3 * w]
    wd = jnp.tanh(s[:, 3 * w:3 * w + LANES])
    ad = s[:, 3 * w + LANES:3 * w + 2 * LANES]
    gd = s[:, 3 * w + 2 * LANES:3 * w + 3 * LANES]
    ones = ones_ref[...]
    kk = k * kk_ref[...]
    nrm = jnp.sqrt(mmcr(kk * kk, ones))
    kk = kk / jnp.maximum(nrm, 1e-12)
    gate = mm1(jax.nn.sigmoid(gd), g2_ref[...])
    ksum = jnp.zeros_like(k)
    for d in range(2):
        zw = w0_ref[d:d + 1, :] + mm1(wd, w2_ref[d])
        w_log = -(jnp.maximum(-zw, 0.0) + jnp.log1p(jnp.exp(-jnp.abs(zw)))) - 0.5
        lw = -jnp.exp(w_log)
        a = jax.nn.sigmoid(a0_ref[d:d + 1, :] + mm1(ad, a2_ref[d]))
        kd = k * (1.0 + (a - 1.0) * ka_ref[...])
        ksum = ksum + kd
        lw_o[d, 0] = lw
        kd_o[d, 0] = kd
        kb_o[d, 0] = kk * a
    r_o[0] = r
    v_o[0] = v
    kk_o[0] = kk
    gate_o[0] = gate
    bonus_o[0] = mmcr(r * ksum * rk_ref[...], ones) * v


def rwkv_prep(p_rw, mu, w0, w2, a0, a2, g2, k_k, k_a, r_k, tb=512):
    b, t, c = p_rw.shape
    tb = min(tb, t)
    nt = t // tb
    w = RW_WIDTH
    z = jnp.zeros((RW_DECAY_LORA, w), F32)
    w2p = jnp.stack([jnp.concatenate([w2[0], z], 0), jnp.concatenate([z, w2[1]], 0)]).astype(BF16)
    a2p = jnp.stack([jnp.concatenate([a2[0], z], 0), jnp.concatenate([z, a2[1]], 0)]).astype(BF16)
    ones = np.kron(np.eye(RW_HEADS, dtype=np.float32), np.ones((RW_DH, RW_DH), np.float32))
    nb8 = t // 8
    hm = jax.ShapeDtypeStruct((b, t, w), F32)
    hm2 = jax.ShapeDtypeStruct((2, b, t, w), F32)
    hm_spec = pl.BlockSpec((1, tb, w), lambda bi, ti: (bi, ti, 0))
    hm2_spec = pl.BlockSpec((2, 1, tb, w), lambda bi, ti: (0, bi, ti, 0))
    return pl.pallas_call(
        _rwkv_prep_kernel,
        grid=(b, nt),
        in_specs=[pl.BlockSpec((1, tb, c), lambda bi, ti: (bi, ti, 0)),
                  pl.BlockSpec((1, 8, c), lambda bi, ti: (bi, jnp.maximum(ti * (tb // 8) - 1, 0), 0)),
                  pl.BlockSpec((1, 8, c), lambda bi, ti: (bi, jnp.minimum((ti + 1) * (tb // 8), nb8 - 1), 0)),
                  _full((2, c)), _full((2, w)), _full((2, LANES, w)), _full((2, w)), _full((2, LANES, w)),
                  _full((RW_GATE_LORA, w)), _full((1, w)), _full((1, w)), _full((1, w)), _full((w, w))],
        out_specs=[hm_spec] * 5 + [hm2_spec] * 3,
        out_shape=[hm] * 5 + [hm2] * 3,
        compiler_params=_params(("parallel", "parallel")),
    )(p_rw, p_rw, p_rw, mu, w0, w2p, a0, a2p, g2.astype(BF16), k_k.reshape(1, w), k_a.reshape(1, w),
      r_k.reshape(1, w), jnp.asarray(ones, BF16))


def _bd(y, hm):
    return jnp.concatenate([y * hm[0], y * hm[1]], axis=0).astype(BF16)


def _bdiag(z, hm):
    return z[:CHUNK] * hm[0] + z[CHUNK:] * hm[1]


def _rwkv_chunks(ins, cum, tri, lmask, eye, hm):
    ell = CHUNK
    nc = range(len(ins))
    cum2 = jnp.concatenate([cum, cum], axis=1)
    cs = []
    for i in range(0, len(ins), 2):
        rhs = jnp.concatenate([jnp.concatenate(_split(x[5]), axis=0) for x in ins[i:i + 2]], axis=1)
        c2 = _dot(cum2, rhs)
        cs += [c2[:, :LANES], c2[:, LANES:]]
    rt = [ins[i][0] * jnp.exp(cs[i][0:ell]) for i in nc]
    at = [ins[i][3] * jnp.exp(cs[i][ell:2 * ell]) for i in nc]
    e_neg = [jnp.exp(-cs[i][0:ell]) for i in nc]
    e_rem = [jnp.exp(cs[i][2 * ell:3 * ell]) for i in nc]
    e_tot = [jnp.exp(cs[i][3 * ell:3 * ell + 1]) for i in nc]
    ar = [jnp.concatenate([at[i], rt[i]], axis=0).astype(BF16) for i in nc]
    pb = [_dot(ar[i], _bd(ins[i][4] * e_neg[i], hm), NT) for i in nc]
    pk = [_dot(ar[i], _bd(ins[i][1] * e_neg[i], hm), NT) for i in nc]
    n = [pb[i][:ell] * tri[0] for i in nc]
    ti = [eye + n[i] * lmask[0] for i in nc]
    for l in range(1, LEVELS):
        p = [_dot((n[i] * lmask[l]).astype(BF16), _bd(ti[i], hm)) for i in nc]
        ti = [ti[i] + _dot(ti[i].astype(BF16), _bd(p[i], hm)) for i in nc]
    vbd = [_bd(ins[i][2], hm) for i in nc]
    mv = [_dot((pk[i][:ell] * tri[0]).astype(BF16), vbd[i]) for i in nc]
    au = [_dot(ti[i].astype(BF16), jnp.concatenate([_bd(at[i], hm), _bd(mv[i], hm)], axis=1)) for i in nc]
    ro = [_dot((pb[i][ell:] * tri[1]).astype(BF16),
               jnp.concatenate([_bd(au[i][:, :LANES], hm), _bd(au[i][:, LANES:], hm)], axis=1)) for i in nc]
    rk = [_dot((pk[i][ell:] * tri[1]).astype(BF16), vbd[i]) for i in nc]
    gh = [_dot((ins[i][4] * e_rem[i]).astype(BF16), au[i].astype(BF16), TN) for i in nc]
    kv = [_dot((ins[i][1] * e_rem[i]).astype(BF16), ins[i][2].astype(BF16), TN) for i in nc]
    out = []
    for i in nc:
        rhat = rt[i] + ro[i][:, :LANES]
        ohat = ro[i][:, LANES:] + rk[i]
        g = eye * e_tot[i] + _bdiag(gh[i][:, :LANES], hm)
        h = _bdiag(gh[i][:, LANES:], hm) + _bdiag(kv[i], hm)
        out.append((rhat, ohat, g, h))
    return out


def _rwkv_state_mm(x, st, hm):
    s_hi = st.astype(BF16).astype(F32)
    bdh, bdl = _bd(s_hi, hm), _bd(st - s_hi, hm)
    xh, xl = _split(x)
    return _dot(jnp.concatenate([xh, xl], axis=1), jnp.concatenate([bdh, bdh], axis=0)) + _dot(xh, bdl)


def _rwkv_kernel(r_ref, v_ref, a_ref, lw_ref, k_ref, b_ref, s0_ref, cum_ref, tri_ref, lm_ref, eye_ref, hm_ref,
                 o_ref, sfin_ref, st_ref, *, rev, nchunk, cu):
    t = pl.program_id(1)
    npair = r_ref.shape[-1] // LANES

    @pl.when(t == 0)
    def _():
        st_ref[...] = s0_ref[0]

    cum = cum_ref[...]
    tri = [tri_ref[0], tri_ref[1]]
    lmask = [lm_ref[l] for l in range(LEVELS)]
    eye = eye_ref[...]
    hm = [hm_ref[0:1, :], hm_ref[1:2, :]]

    def body(it, carry):
        sts = [st_ref[p] for p in range(npair)]
        rows, ins = [], []
        for j in range(cu):
            ci = it * cu + j
            c = (nchunk - 1 - ci) if rev else ci
            rows.append(pl.ds(pl.multiple_of(c * CHUNK, CHUNK), CHUNK))
            for p in range(npair):
                ls = slice(p * LANES, (p + 1) * LANES)
                ins.append((r_ref[0, rows[j], ls], k_ref[0, 0, rows[j], ls], v_ref[0, rows[j], ls],
                            -a_ref[0, rows[j], ls], b_ref[0, 0, rows[j], ls], lw_ref[0, 0, rows[j], ls]))
        preps = _rwkv_chunks(ins, cum, tri, lmask, eye, hm)
        outs = []
        for j in range(cu):
            for p in range(npair):
                rhat, ohat, g, hh = preps[j * npair + p]
                res = _rwkv_state_mm(jnp.concatenate([rhat, g], axis=0), sts[p], hm)
                outs.append(res[:CHUNK] + ohat)
                sts[p] = res[CHUNK:] + hh
        for j in range(cu):
            for p in range(npair):
                o_ref[0, rows[j], p * LANES:(p + 1) * LANES] = outs[j * npair + p]
        for p in range(npair):
            st_ref[p] = sts[p]
        return carry

    lax.fori_loop(0, nchunk // cu, body, 0)

    @pl.when(t == pl.num_programs(1) - 1)
    def _():
        sfin_ref[0] = st_ref[...]


def rwkv_scan(rev, d, r, v, kk, lw, kd, kb, s0, tb=512, cu=4):
    b, t, w = r.shape
    tb = min(tb, t)
    nt = t // tb
    nchunk = tb // CHUNK
    cu = min(cu, nchunk)
    consts = _rwkv_consts(rev)

    def tmap(i):
        return (nt - 1 - i) if rev else i

    spec = pl.BlockSpec((1, tb, w), lambda bi, ti: (bi, tmap(ti), 0))
    spec2 = pl.BlockSpec((1, 1, tb, w), lambda bi, ti: (d, bi, tmap(ti), 0))
    st_spec = pl.BlockSpec((1,) + s0.shape[1:], lambda bi, ti: (bi, 0, 0, 0))
    return pl.pallas_call(
        functools.partial(_rwkv_kernel, rev=rev, nchunk=nchunk, cu=cu),
        grid=(b, nt),
        in_specs=[spec, spec, spec, spec2, spec2, spec2, st_spec] + [_full(c.shape) for c in consts],
        out_specs=[spec, st_spec],
        out_shape=[jax.ShapeDtypeStruct(r.shape, F32), jax.ShapeDtypeStruct(s0.shape, F32)],
        scratch_shapes=[pltpu.VMEM(s0.shape[1:], F32)],
        compiler_params=_params(("parallel", "arbitrary")),
    )(r, v, kk, lw, kd, kb, s0, *consts)


def _even_out_kernel(x_ref, g1_ref, of_ref, ob_ref, gh_ref, ng_ref, rf_ref, rb_ref, bonus_ref, gate_ref,
                     lng_ref, lnb_ref, ones_ref, w_ref, o_ref):
    z = of_ref[0] + ob_ref[0]
    gh = gh_ref[0]
    acc = None
    for h in range(HG_HEADS):
        ls = slice(h * HG_DV, (h + 1) * HG_DV)
        zh = z[:, ls]
        yh = zh * lax.rsqrt(jnp.mean(zh * zh, axis=-1, keepdims=True) + EPS) * ng_ref[...] * _silu(gh[:, ls])
        part = mm1(yh, w_ref[ls, :])
        acc = part if acc is None else acc + part
    o = rf_ref[0] + rb_ref[0]
    ones = ones_ref[...]
    mean = mmcr(o, ones) * (1.0 / RW_DH)
    cen = o - mean
    var = mmcr(cen * cen, ones) * (1.0 / RW_DH)
    o = cen * lax.rsqrt(var + RW_GN_EPS) * lng_ref[...] + lnb_ref[...]
    o = (o + bonus_ref[0]) * gate_ref[0]
    acc = acc + mm1(o, w_ref[HG_WIDTH:, :])
    o_ref[0] = x_ref[0] + g1_ref[0] * acc


def even_out(x, g1, o_f, o_b, p_hg, hg_g, rw_f, rw_b, bonus, gate, ln_g, ln_b, w_out, tm=512):
    b, t, d = x.shape
    tm = min(tm, t)
    tok = lambda w: pl.BlockSpec((1, tm, w), lambda bi, ti: (bi, ti, 0))
    hm = tok(RW_WIDTH)
    ones = np.kron(np.eye(RW_HEADS, dtype=np.float32), np.ones((RW_DH, RW_DH), np.float32))
    return pl.pallas_call(
        _even_out_kernel,
        grid=(b, t // tm),
        in_specs=[tok(d), pl.BlockSpec((1, 1, d), lambda bi, ti: (bi, 0, 0)), tok(HG_WIDTH), tok(HG_WIDTH),
                  pl.BlockSpec((1, tm, HG_WIDTH), lambda bi, ti: (bi, ti, 4)), _full((1, HG_DV)),
                  hm, hm, hm, hm, _full((1, RW_WIDTH)), _full((1, RW_WIDTH)), _full(ones.shape),
                  pl.BlockSpec(w_out.shape, lambda bi, ti: (0, 0), pipeline_mode=pl.Buffered(1))],
        out_specs=tok(d),
        out_shape=jax.ShapeDtypeStruct(x.shape, F32),
        compiler_params=_params(("parallel", "parallel")),
    )(x, g1.reshape(b, 1, d), o_f, o_b, p_hg, hg_g.reshape(1, HG_DV), rw_f, rw_b, bonus, gate,
      ln_g.reshape(1, RW_WIDTH), ln_b.reshape(1, RW_WIDTH), jnp.asarray(ones, BF16), w_out.astype(BF16))


def _cpow_table(zr, zi, n):
    def step(c, _):
        cr, ci = c
        return (cr * zr - ci * zi, cr * zi + ci * zr), (cr, ci)
    (_, _), (pr, pi) = lax.scan(step, (jnp.ones_like(zr), jnp.zeros_like(zr)), None, length=n)
    return pr, pi


def _s5_params(a_re, a_im, log_dt, b_re, b_im, c_re, c_im, rev, nsteps):
    ell = S5_CHUNK
    dt = jnp.exp(log_dt)[:, None]
    mag = jnp.exp(a_re * dt)
    lr, li = mag * jnp.cos(a_im * dt), mag * jnp.sin(a_im * dt)
    den = a_re * a_re + a_im * a_im
    fr = ((lr - 1.0) * a_re + li * a_im) / den
    fi = (li * a_re - (lr - 1.0) * a_im) / den
    bbr = fr[..., None] * b_re - fi[..., None] * b_im
    bbi = fr[..., None] * b_im + fi[..., None] * b_re
    pr, pi = _cpow_table(lr, li, ell + 1)
    pr_t, pi_t = jnp.transpose(pr, (1, 2, 0)), jnp.transpose(pi, (1, 2, 0))
    cre_t, cim_t = jnp.swapaxes(c_re, 1, 2), jnp.swapaxes(c_im, 1, 2)
    cr_p = cre_t[:, :, None, :] * pr_t[..., None] - cim_t[:, :, None, :] * pi_t[..., None]
    ci_p = cre_t[:, :, None, :] * pi_t[..., None] + cim_t[:, :, None, :] * pr_t[..., None]
    kern = (jnp.einsum('gntc,gni->gitc', cr_p[:, :, :ell], bbr)
            - jnp.einsum('gntc,gni->gitc', ci_p[:, :, :ell], bbi))
    s_idx = np.arange(ell)[:, None]
    t_idx = np.arange(ell)[None, :]
    lag = (s_idx - t_idx) if rev else (t_idx - s_idx)
    valid = jnp.asarray(lag >= 0)
    kt = jnp.take(kern, np.clip(lag, 0, ell - 1).reshape(-1), axis=2)
    kt = jnp.where(valid[None, None, :, :, None], kt.reshape(S5_GROUPS, S5_GROUP, ell, ell, S5_GROUP), 0.0)
    toep = jnp.transpose(kt, (0, 2, 1, 3, 4)).reshape(S5_GROUPS, S5_FEAT, S5_FEAT)
    e_idx = (np.arange(ell) if rev else (ell - 1 - np.arange(ell)))
    pwr, pwi = jnp.swapaxes(pr[e_idx], 0, 1), jnp.swapaxes(pi[e_idx], 0, 1)
    bbr_t, bbi_t = jnp.swapaxes(bbr, 1, 2), jnp.swapaxes(bbi, 1, 2)
    p_re = pwr[:, :, None, :] * bbr_t[:, None] - pwi[:, :, None, :] * bbi_t[:, None]
    p_im = pwr[:, :, None, :] * bbi_t[:, None] + pwi[:, :, None, :] * bbr_t[:, None]
    p_re = p_re.reshape(S5_GROUPS, S5_FEAT, S5_STATE)
    p_im = p_im.reshape(S5_GROUPS, S5_FEAT, S5_STATE)
    q_sl = slice(ell, 0, -1) if rev else slice(1, ell + 1)
    q_re = cr_p[:, :, q_sl].reshape(S5_GROUPS, S5_STATE, S5_FEAT)
    q_im = -ci_p[:, :, q_sl].reshape(S5_GROUPS, S5_STATE, S5_FEAT)
    zr, zi = [pr[ell]], [pi[ell]]
    for _ in range(nsteps - 1):
        zr, zi = zr + [zr[-1] * zr[-1] - zi[-1] * zi[-1]], zi + [2.0 * zr[-1] * zi[-1]]
    zr = jnp.stack(zr)[:, :, None, :]
    zi = jnp.stack(zi)[:, :, None, :]
    return toep, p_re, p_im, q_re, q_im, zr, zi


def _s5_expand(toep, p_re, p_im, q_re, q_im, zr, zi):
    ng = LANES // S5_GROUP
    nc = S5_GROUPS // ng
    ell, grp, st = S5_CHUNK, S5_GROUP, S5_STATE
    eye = jnp.eye(ng, dtype=BF16)
    kw = ell * LANES
    w = None if toep is None else (
        toep.astype(BF16).reshape(nc, ng, ell * grp, ell, 1, grp) * eye[None, :, None, None, :, None]
    ).reshape(nc, kw, kw)
    pe = lambda p: (p.astype(BF16).reshape(nc, ng, ell * grp, 1, st) * eye[None, :, None, :, None]
                    ).reshape(nc, kw, ng * st)
    qe = lambda q: (q.astype(BF16).reshape(nc, ng, st, ell, 1, grp) * eye[None, :, None, None, :, None]
                    ).reshape(nc, ng * st, kw)
    p = jnp.concatenate([pe(p_re), pe(p_im)], axis=2)
    q = jnp.concatenate([qe(q_re), qe(q_im)], axis=1)
    ns = zr.shape[0]
    return w, p, q, zr.reshape(ns, nc, 1, ng * st), zi.reshape(ns, nc, 1, ng * st)


def _s5_kernel(*refs, rev, nsteps, intra):
    if intra:
        u_ref, w_ref, p_ref, q_ref, zr_ref, zi_ref, h0r_ref, h0i_ref, y_ref, hfr_ref, hfi_ref, cr_ref, ci_ref = refs
    else:
        u_ref, p_ref, q_ref, zr_ref, zi_ref, h0r_ref, h0i_ref, y_ref, hfr_ref, hfi_ref, cr_ref, ci_ref = refs
    t = pl.program_id(2)

    @pl.when(t == 0)
    def _():
        cr_ref[...] = h0r_ref[0, 0]
        ci_ref[...] = h0i_ref[0, 0]

    ng = LANES // S5_GROUP
    jb = u_ref.shape[2] // S5_CHUNK
    pieces = [u_ref[0, 0, pl.ds(s, jb, stride=S5_CHUNK), :] for s in range(S5_CHUNK)]
    slot = lax.broadcasted_iota(jnp.int32, (1, LANES), 1) // S5_GROUP
    cols = []
    for g in range(ng):
        for h in range(S5_CHUNK // ng):
            acc = None
            for sp in range(ng):
                shift = ((sp - g) * S5_GROUP) % LANES
                piece = pieces[h * ng + sp]
                rolled = piece if shift == 0 else pltpu.roll(piece, shift, axis=1)
                acc = rolled if acc is None else jnp.where(slot == sp, rolled, acc)
            cols.append(acc.astype(BF16))
    u = jnp.concatenate(cols, axis=1)
    half = p_ref.shape[-1] // 2
    row = lax.broadcasted_iota(jnp.int32, (jb, 1), 0)
    edge = (jb - 1) if rev else 0
    x = _dot(u, p_ref[0])
    xr, xi = x[:, :half], x[:, half:]
    car_r, car_i = cr_ref[...], ci_ref[...]
    z1r, z1i = zr_ref[0, 0], zi_ref[0, 0]
    at_edge = row == edge
    hr = xr + jnp.where(at_edge, z1r * car_r - z1i * car_i, 0.0)
    hi = xi + jnp.where(at_edge, z1r * car_i + z1i * car_r, 0.0)
    for s in range(nsteps):
        sh = 1 << s
        if sh >= jb:
            break
        zr, zi = zr_ref[s, 0], zi_ref[s, 0]
        if rev:
            pr_ = jnp.where(row < jb - sh, pltpu.roll(hr, jb - sh, axis=0), 0.0)
            pi_ = jnp.where(row < jb - sh, pltpu.roll(hi, jb - sh, axis=0), 0.0)
        else:
            pr_ = jnp.where(row >= sh, pltpu.roll(hr, sh, axis=0), 0.0)
            pi_ = jnp.where(row >= sh, pltpu.roll(hi, sh, axis=0), 0.0)
        hr, hi = hr + (zr * pr_ - zi * pi_), hi + (zr * pi_ + zi * pr_)
    if rev:
        hpr = jnp.where(at_edge, car_r, pltpu.roll(hr, jb - 1, axis=0))
        hpi = jnp.where(at_edge, car_i, pltpu.roll(hi, jb - 1, axis=0))
        cr_ref[...] = hr[0:1]
        ci_ref[...] = hi[0:1]
    else:
        hpr = jnp.where(at_edge, car_r, pltpu.roll(hr, 1, axis=0))
        hpi = jnp.where(at_edge, car_i, pltpu.roll(hi, 1, axis=0))
        cr_ref[...] = hr[jb - 1:jb]
        ci_ref[...] = hi[jb - 1:jb]
    y = _dot(jnp.concatenate([hpr, hpi], axis=1).astype(BF16), q_ref[0])
    if intra:
        y = y + _dot(u, w_ref[0])
    for s in range(S5_CHUNK):
        y_ref[0, 0, pl.ds(s, jb, stride=S5_CHUNK), :] = y[:, s * LANES:(s + 1) * LANES]

    @pl.when(t == pl.num_programs(2) - 1)
    def _():
        hfr_ref[0, 0] = cr_ref[...]
        hfi_ref[0, 0] = ci_ref[...]


def s5_dir(rev, u4, w, p, q, zr, zi, h0r, h0i, jb=S5_ROWS):
    nc, nb, t, _ = u4.shape
    j = t // S5_CHUNK
    jb = min(jb, j)
    nj = j // jb
    nsteps = max(int(math.log2(jb)), 1)
    nst = h0r.shape[-1]

    def tmap(ti):
        return (nj - 1 - ti) if rev else ti

    const = lambda a: pl.BlockSpec((1,) + a.shape[1:], lambda ci, bi, ti: (ci, 0, 0), pipeline_mode=pl.Buffered(1))
    hspec = pl.BlockSpec((1, 1, 1, nst), lambda ci, bi, ti: (ci, bi, 0, 0))
    zspec = pl.BlockSpec((zr.shape[0], 1, 1, nst), lambda ci, bi, ti: (0, ci, 0, 0))
    uspec = pl.BlockSpec((1, 1, jb * S5_CHUNK, LANES), lambda ci, bi, ti: (ci, bi, tmap(ti), 0))
    mats = ([w] if w is not None else []) + [p, q]
    y, hfr, hfi = pl.pallas_call(
        functools.partial(_s5_kernel, rev=rev, nsteps=nsteps, intra=w is not None),
        grid=(nc, nb, nj),
        in_specs=[uspec] + [const(m) for m in mats] + [zspec, zspec, hspec, hspec],
        out_specs=[uspec, hspec, hspec],
        out_shape=[jax.ShapeDtypeStruct(u4.shape, F32), jax.ShapeDtypeStruct(h0r.shape, F32),
                   jax.ShapeDtypeStruct(h0i.shape, F32)],
        scratch_shapes=[pltpu.VMEM((1, nst), F32), pltpu.VMEM((1, nst), F32)],
        compiler_params=_params(("parallel", "parallel", "arbitrary")),
    )(u4, *mats, zr, zi, h0r, h0i)
    return y, hfr, hfi


def _gelu_tanh(x):
    return 0.5 * x * (1.0 + jnp.tanh(math.sqrt(2.0 / math.pi) * (x + 0.044715 * (x * x * x))))


def _odd_out_kernel(x_ref, g1_ref, of_ref, ob_ref, gg_ref, ng_ref, yf_ref, yb_ref, u_ref, dsk_ref,
                    wglu_ref, bglu_ref, w_ref, o_ref):
    z = of_ref[0] + ob_ref[0]
    gg = gg_ref[0]
    acc = None
    for h in range(GLA_HEADS):
        ls = slice(h * GLA_DV, (h + 1) * GLA_DV)
        zh = z[:, ls]
        yh = zh * lax.rsqrt(jnp.mean(zh * zh, axis=-1, keepdims=True) + EPS) * ng_ref[...] * _silu(gg[:, ls])
        part = mm1(yh, w_ref[ls, :])
        acc = part if acc is None else acc + part
    cat = lambda r: jnp.concatenate([r[c, 0] for c in range(r.shape[0])], axis=1)
    y = (cat(yf_ref) + cat(yb_ref)) + dsk_ref[...] * cat(u_ref)
    zz = _gelu_tanh(y)
    o_s5 = zz * jax.nn.sigmoid(mm1(zz, wglu_ref[...]) + bglu_ref[...])
    acc = acc + mm1(o_s5, w_ref[GLA_WIDTH:, :])
    o_ref[0] = x_ref[0] + g1_ref[0] * acc


def odd_out(x, g1, o_f, o_b, g_gla, gla_g, y_f, y_b, u, d_skip, w_glu, b_glu, w_out, tm=512):
    b, t, d = x.shape
    tm = min(tm, t)
    tok = lambda w: pl.BlockSpec((1, tm, w), lambda bi, ti: (bi, ti, 0))
    const = lambda a: pl.BlockSpec(a.shape, lambda bi, ti: (0, 0), pipeline_mode=pl.Buffered(1))
    blk = pl.BlockSpec((S5_WIDTH // LANES, 1, tm, LANES), lambda bi, ti: (0, bi, ti, 0))
    return pl.pallas_call(
        _odd_out_kernel,
        grid=(b, t // tm),
        in_specs=[tok(d), pl.BlockSpec((1, 1, d), lambda bi, ti: (bi, 0, 0)), tok(GLA_WIDTH), tok(GLA_WIDTH),
                  tok(GLA_WIDTH), _full((1, GLA_DV)), blk, blk, blk,
                  _full((1, S5_WIDTH)), const(w_glu), _full((1, S5_WIDTH)), const(w_out)],
        out_specs=tok(d),
        out_shape=jax.ShapeDtypeStruct(x.shape, F32),
        compiler_params=_params(("parallel", "parallel")),
    )(x, g1.reshape(b, 1, d), o_f, o_b, g_gla, gla_g.reshape(1, GLA_DV), y_f, y_b, u,
      d_skip.reshape(1, S5_WIDTH), w_glu.astype(BF16), b_glu.reshape(1, S5_WIDTH), w_out.astype(BF16))


def _ffn_kernel(x_ref, xp_ref, xn_ref, ng_ref, sc_ref, sh_ref, g2_ref, wup_ref, cw_ref, cb_ref, wdn_ref,
                fg_ref, o_ref, *, cols, fw, ahead, final):
    r = pl.program_id(1)
    nr = pl.num_programs(1)
    x = x_ref[0]
    n = x.shape[0]

    def modnorm(z):
        h = z * lax.rsqrt(jnp.mean(z * z, axis=-1, keepdims=True) + EPS) * ng_ref[...]
        return (h * (1.0 + sc_ref[0]) + sh_ref[0]).astype(BF16)

    h_main = modnorm(x)
    h_prev = jnp.where(r > 0, modnorm(xp_ref[0]), jnp.zeros((), BF16))
    h_next = jnp.where(r < nr - 1, modnorm(xn_ref[0]), jnp.zeros((), BF16))
    h_all = jnp.concatenate([h_prev, h_main, h_next], axis=0)
    pos = lax.broadcasted_iota(jnp.int32, (n, 1), 0)
    colid = jnp.bitwise_and(pos, cols - 1)
    first_col, last_col = colid == 0, colid == cols - 1
    slices = list(range(0, D_FF, fw))

    def up(f0):
        return (_dot(h_all, wup_ref[:, f0:f0 + fw]),
                _dot(h_main, wup_ref[:, D_FF + f0:D_FF + f0 + fw]))

    acc = None
    ups = [up(f0) for f0 in slices[:ahead]]
    for i, f0 in enumerate(slices):
        gate, val = ups[i]
        if i + ahead < len(slices):
            ups.append(up(slices[i + ahead]))
        cw = lambda tap: cw_ref[tap:tap + 1, f0:f0 + fw]
        rows3 = [gate[di * cols:di * cols + n] for di in range(3)]
        left = rows3[0] * cw(0) + rows3[1] * cw(3) + rows3[2] * cw(6)
        mid = rows3[0] * cw(1) + rows3[1] * cw(4) + rows3[2] * cw(7)
        right = rows3[0] * cw(2) + rows3[1] * cw(5) + rows3[2] * cw(8)
        conv = (cb_ref[:, f0:f0 + fw] + mid
                + jnp.where(first_col, 0.0, pltpu.roll(left, 1, axis=0))
                + jnp.where(last_col, 0.0, pltpu.roll(right, n - 1, axis=0)))
        act = (_silu(conv) * val).astype(BF16)
        part = _dot(act, wdn_ref[f0:f0 + fw, :])
        acc = part if acc is None else acc + part
    y = x + g2_ref[0] * acc
    if final:
        y = y * lax.rsqrt(jnp.mean(y * y, axis=-1, keepdims=True) + EPS) * fg_ref[...]
    o_ref[0] = y


def conv_ffn(x, ng, sc, sh, g2, w_up, conv_w, conv_b, w_down, final_g, rows, cols, final, rt=16, fw=256,
             ahead=3):
    b, t, d = x.shape
    rt = min(rt, rows)
    nr = rows // rt
    n = rt * cols
    vec = lambda a: pl.BlockSpec((1, 1, d), lambda bi, ri: (bi, 0, 0))
    const = lambda a: pl.BlockSpec(a.shape, lambda bi, ri: (0, 0), pipeline_mode=pl.Buffered(1))
    wup = w_up.astype(BF16)
    wdn = w_down.astype(BF16)
    cw = conv_w.reshape(9, D_FF)
    cb = conv_b.reshape(1, D_FF)
    return pl.pallas_call(
        functools.partial(_ffn_kernel, cols=cols, fw=fw, ahead=ahead, final=final),
        grid=(b, nr),
        in_specs=[pl.BlockSpec((1, n, d), lambda bi, ri: (bi, ri, 0)),
                  pl.BlockSpec((1, cols, d), lambda bi, ri: (bi, jnp.maximum(ri * rt - 1, 0), 0)),
                  pl.BlockSpec((1, cols, d), lambda bi, ri: (bi, jnp.minimum((ri + 1) * rt, rows - 1), 0)),
                  _full((1, d)), vec(sc), vec(sh), vec(g2), const(wup), _full((9, D_FF)), _full((1, D_FF)),
                  const(wdn), _full((1, d))],
        out_specs=pl.BlockSpec((1, n, d), lambda bi, ri: (bi, ri, 0)),
        out_shape=jax.ShapeDtypeStruct(x.shape, F32),
        compiler_params=_params(("parallel", "parallel")),
    )(x, x, x, ng.reshape(1, d), sc.reshape(b, 1, d), sh.reshape(b, 1, d), g2.reshape(b, 1, d), wup, cw, cb,
      wdn, final_g.reshape(1, d))


def _even_layer(hx_args, ctx_args, prm, with_ctx_out):
    (w_in, w_out, lb, hg_g, mu, w0, w2, a0, a2, g2, k_k, k_a, r_k, ln_g, ln_b) = prm
    hg_cols = 5 * HG_WIDTH
    res = {}
    bsz = hx_args[0].shape[0]
    s_hg = [jnp.zeros((bsz, HG_HEADS, HG_DV, LANES), F32)] * 2
    s_rw = [jnp.zeros((bsz, RW_WIDTH // LANES, RW_DH, LANES), F32)] * 2
    for name, (x, g, sc, sh, g1), want_out in (("ctx", ctx_args, with_ctx_out), ("x", hx_args, True)):
        p_hg, p_rw = norm_proj(x, g, sc, sh, w_in, (hg_cols, RW_COLS))
        r, v, kk, gate, bonus, lw, kd, kb = rwkv_prep(p_rw, mu, w0, w2, a0, a2, g2, k_k, k_a, r_k)
        o_hg, o_rw = [], []
        for d in range(2):
            o, s_hg[d] = gated_scan("hgrn", d == 1, (p_hg, lb[d:d + 1], d), s_hg[d])
            o_hg.append(o)
            o, s_rw[d] = rwkv_scan(d == 1, d, r, v, kk, lw, kd, kb, s_rw[d])
            o_rw.append(o)
        if want_out:
            res[name] = even_out(x, g1, o_hg[0], o_hg[1], p_hg, hg_g, o_rw[0], o_rw[1], bonus, gate,
                                 ln_g, ln_b, w_out)
    return res


def _odd_layer(hx_args, ctx_args, prm, with_ctx_out):
    (w_in, w_out, gla_w2, gla_b, gla_g, a_re, a_im, log_dt, b_re, b_im, c_re, c_im, d_skip, w_glu, b_glu) = prm
    nq = 2 * GLA_QK + 2 * GLA_WIDTH
    w_perm = jnp.concatenate([w_in[:, :nq], w_in[:, nq + 2 * GLA_GATE_LORA:], w_in[:, nq:nq + 2 * GLA_GATE_LORA]], 1)
    z16 = jnp.zeros((GLA_GATE_LORA, GLA_QK), F32)
    w2pad = [jnp.concatenate([gla_w2[0], z16], 0).astype(BF16), jnp.concatenate([z16, gla_w2[1]], 0).astype(BF16)]
    res = {}
    bsz = hx_args[0].shape[0]
    s_gla = [jnp.zeros((bsz, GLA_HEADS, GLA_DV, LANES), F32)] * 2
    zs = jnp.zeros((S5_WIDTH // LANES, bsz, 1, (LANES // S5_GROUP) * S5_STATE), F32)
    s_s5 = [(zs, zs), (zs, zs)]
    s5_mats = []
    for d in range(2):
        toep, *rest = _s5_params(a_re[d], a_im[d], log_dt[d], b_re[d], b_im[d], c_re[d], c_im[d],
                                 rev=d == 1, nsteps=S5_MAX_STEPS)
        s5_mats.append((toep, *rest))
    s5_mats = [_s5_expand(s5_mats[0][0] + s5_mats[1][0], *s5_mats[0][1:]), _s5_expand(None, *s5_mats[1][1:])]
    for name, (x, g, sc, sh, g1), want_out in (("ctx", ctx_args, with_ctx_out), ("x", hx_args, True)):
        q, k, v, gg, u, gd = norm_proj(x, g, sc, sh, w_perm,
                                       (GLA_QK, GLA_QK, GLA_WIDTH, GLA_WIDTH, S5_WIDTH, 2 * GLA_GATE_LORA),
                                       blocked=(4,))
        o_gla, y_s5 = [], []
        for d in range(2):
            o, s_gla[d] = gated_scan("gla", d == 1, (q, k, v, gd, w2pad[d], gla_b[d].reshape(1, GLA_QK)), s_gla[d],
                                     cu=4)
            o_gla.append(o)
            w_d, p_d, q_d, zr_d, zi_d = s5_mats[d]
            y, hr, hi = s5_dir(d == 1, u, w_d if d == 0 else None, p_d, q_d, zr_d, zi_d, s_s5[d][0], s_s5[d][1])
            s_s5[d] = (hr, hi)
            y_s5.append(y)
        if want_out:
            res[name] = odd_out(x, g1, o_gla[0], o_gla[1], gg, gla_g, y_s5[0], y_s5[1], u, d_skip, w_glu, b_glu,
                                w_out)
    return res


def kernel(x, c, ctx, c_ctx, ada_w, ada_b, norm1_g, norm2_g, final_g,
           ev_w_in, ev_w_out, hg_lb, hg_norm_g, rw_mu, rw_w0, rw_w2, rw_a0, rw_a2, rw_g2,
           rw_k_k, rw_k_a, rw_r_k, rw_ln_g, rw_ln_b,
           od_w_in, od_w_out, gla_w2, gla_b, gla_norm_g, s5_a_re, s5_a_im, s5_log_dt,
           s5_b_re, s5_b_im, s5_c_re, s5_c_im, s5_d, s5_w_glu, s5_b_glu,
           ffn_w_up, ffn_conv_w, ffn_conv_b, ffn_w_down):
    bsz, seq, d = x.shape
    depth = ada_w.shape[0]
    rows = seq // GRID_W
    ctx_len = ctx.shape[1]
    lb_all = jnp.cumsum(jax.nn.softmax(hg_lb.astype(F32), axis=0), axis=0)

    for layer in range(depth):
        last = layer == depth - 1
        j = layer // 2
        mod = jnp.split(jax.nn.silu(c) @ ada_w[layer] + ada_b[layer], 6, axis=-1)
        mod_c = jnp.split(jax.nn.silu(c_ctx) @ ada_w[layer] + ada_b[layer], 6, axis=-1)
        sh1, sc1, g1, sh2, sc2, g2 = mod
        csh1, csc1, cg1, csh2, csc2, cg2 = [jnp.broadcast_to(m[None], (bsz, d)) for m in mod_c]
        hx_args = (x, norm1_g[layer], sc1, sh1, g1)
        ctx_args = (ctx, norm1_g[layer], csc1, csh1, cg1)
        if layer % 2 == 0:
            prm = (ev_w_in[j], ev_w_out[j], lb_all[j], hg_norm_g[j], rw_mu[j], rw_w0[j], rw_w2[j], rw_a0[j],
                   rw_a2[j], rw_g2[j], rw_k_k[j], rw_k_a[j], rw_r_k[j], rw_ln_g[j], rw_ln_b[j])
            res = _even_layer(hx_args, ctx_args, prm, not last)
        else:
            prm = (od_w_in[j], od_w_out[j], gla_w2[j], gla_b[j], gla_norm_g[j], s5_a_re[j], s5_a_im[j],
                   s5_log_dt[j], s5_b_re[j], s5_b_im[j], s5_c_re[j], s5_c_im[j], s5_d[j], s5_w_glu[j], s5_b_glu[j])
            res = _odd_layer(hx_args, ctx_args, prm, not last)
        ffn = (ffn_w_up[layer], ffn_conv_w[layer], ffn_conv_b[layer], ffn_w_down[layer], final_g)
        x = conv_ffn(res["x"], norm2_g[layer], sc2, sh2, g2, *ffn, rows, GRID_W, final=last)
        if not last:
            ctx = conv_ffn(res["ctx"], norm2_g[layer], csc2, csh2, cg2, *ffn, 1, ctx_len, final=False)
    return x
```

```python
import functools
import math

import numpy as np
import jax
import jax.numpy as jnp
from jax import lax
from jax.experimental import pallas as pl
from jax.experimental.pallas import tpu as pltpu

F32 = jnp.float32
BF16 = jnp.bfloat16

EPS = 1e-6
GRID_W = 64
CHUNK = 64
LEVELS = 6

HG_HEADS, HG_DK, HG_DV = 4, 128, 128
HG_WIDTH = HG_HEADS * HG_DV
RW_HEADS, RW_DH = 8, 64
RW_WIDTH = RW_HEADS * RW_DH
RW_DECAY_LORA, RW_AAA_LORA, RW_GATE_LORA = 64, 64, 128
RW_GN_EPS = 64e-5
RW_COLS = 3 * RW_WIDTH + 2 * RW_DECAY_LORA + 2 * RW_AAA_LORA + RW_GATE_LORA
GLA_HEADS, GLA_DK, GLA_DV = 4, 64, 128
GLA_WIDTH = GLA_HEADS * GLA_DV
GLA_QK = GLA_HEADS * GLA_DK
GLA_GATE_LORA = 16
GLA_TAU = 16.0
S5_WIDTH, S5_GROUP, S5_STATE = 512, 16, 64
S5_GROUPS = S5_WIDTH // S5_GROUP
S5_CHUNK = 16
S5_FEAT = S5_CHUNK * S5_GROUP
S5_ROWS = 256
S5_MAX_STEPS = S5_ROWS.bit_length() - 1
D_FF = 2816

LANES = 128
VMEM_LIMIT = 56 * 1024 * 1024

NN = ((1,), (0,))
NT = ((1,), (1,))
TN = ((0,), (0,))


def _dot(a, b, dims=NN):
    return lax.dot_general(a, b, (dims, ((), ())), preferred_element_type=F32)


def _split(a):
    hi = a.astype(BF16)
    return hi, (a - hi.astype(F32)).astype(BF16)


def mm1(a, b, dims=NN):
    return _dot(a.astype(BF16), b.astype(BF16), dims)


def mm3(a, b, dims=NN):
    ah, al = _split(a)
    bh, bl = _split(b)
    return _dot(ah, bh, dims) + (_dot(al, bh, dims) + _dot(ah, bl, dims))


def mmc(c, x, dims=NN):
    xh, xl = _split(x)
    return _dot(c, xh, dims) + _dot(c, xl, dims)


def mmcr(x, c, dims=NN):
    xh, xl = _split(x)
    return _dot(xh, c, dims) + _dot(xl, c, dims)


def _silu(x):
    return x * jax.nn.sigmoid(x)


def _params(sem):
    return pltpu.CompilerParams(dimension_semantics=sem, vmem_limit_bytes=VMEM_LIMIT)


def _full(shape):
    nd = len(shape)
    return pl.BlockSpec(shape, lambda *_: (0,) * nd)


def _tau(rev):
    t = np.arange(CHUNK)
    return (CHUNK - 1 - t) if rev else t


def _level_masks(rev):
    tau = _tau(rev)
    ti, si = tau[:, None], tau[None, :]
    out = []
    for l in range(LEVELS):
        m = (((ti >> l) & 1) == 1) & (((si >> l) & 1) == 0) & ((ti >> (l + 1)) == (si >> (l + 1)))
        out.append(m)
    return np.stack(out).astype(np.float32)


def _gls_consts(rev):
    tau = _tau(rev)
    ti, ii = tau[:, None], tau[None, :]
    mats = [ii <= ti, ii > ti]
    sel = []
    for l in range(LEVELS):
        bit = (ti >> l) & 1
        start = (ti >> l) << l
        end = start + (1 << l) - 1
        mats.append(np.where(bit == 1, (ii >= start) & (ii <= ti), (ii > ti) & (ii <= end)))
        sel.append(np.broadcast_to(bit, (CHUNK, LANES)))
    mats.append(np.ones((8, CHUNK), bool))
    cstack = np.concatenate(mats, 0).astype(np.float32)
    return (jnp.asarray(cstack, BF16), jnp.asarray(np.stack(sel).astype(np.float32)),
            jnp.asarray(_level_masks(rev)), jnp.asarray(np.eye(CHUNK, dtype=np.float32)))


def _rwkv_consts(rev):
    tau = _tau(rev)
    ti, ii = tau[:, None], tau[None, :]
    cum = np.concatenate([ii <= ti, ii < ti, ii > ti, np.ones((8, CHUNK), bool)], 0).astype(np.float32)
    tri = np.stack([ii < ti, ii <= ti]).astype(np.float32)
    pair = lambda m: np.concatenate([m, m], axis=-1)
    hmask = np.zeros((2, LANES), np.float32)
    hmask[0, :RW_DH] = 1.0
    hmask[1, RW_DH:] = 1.0
    return (jnp.asarray(cum, BF16), jnp.asarray(pair(tri)), jnp.asarray(pair(_level_masks(rev))),
            jnp.asarray(pair(np.eye(CHUNK, dtype=np.float32))), jnp.asarray(hmask))


def _norm_proj_kernel(x_ref, g_ref, sc_ref, sh_ref, w_ref, *o_refs, splits, blocked):
    x = x_ref[...]
    h = x * lax.rsqrt(jnp.mean(x * x, axis=-1, keepdims=True) + EPS) * g_ref[...]
    h = (h * (1.0 + sc_ref[0]) + sh_ref[0]).astype(BF16)
    off = 0
    for i, (o_ref, n) in enumerate(zip(o_refs, splits)):
        res = _dot(h, w_ref[:, off:off + n])
        if i in blocked:
            for c in range(n // LANES):
                o_ref[c] = res[:, c * LANES:(c + 1) * LANES]
        else:
            o_ref[...] = res
        off += n


def norm_proj(x, g, sc, sh, w, splits, blocked=(), tm=512):
    b, t, d = x.shape
    tm = min(tm, t)
    steps = t // tm
    n = w.shape[1]
    out_specs, out_shape = [], []
    for i, s in enumerate(splits):
        if i in blocked:
            out_specs.append(pl.BlockSpec((s // LANES, tm, LANES), lambda i: (0, i, 0)))
            out_shape.append(jax.ShapeDtypeStruct((s // LANES, b * t, LANES), F32))
        else:
            out_specs.append(pl.BlockSpec((tm, s), lambda i: (i, 0)))
            out_shape.append(jax.ShapeDtypeStruct((b * t, s), F32))
    outs = pl.pallas_call(
        functools.partial(_norm_proj_kernel, splits=tuple(splits), blocked=tuple(blocked)),
        grid=(b * steps,),
        in_specs=[pl.BlockSpec((tm, d), lambda i: (i, 0)),
                  _full((1, d)),
                  pl.BlockSpec((1, 1, d), lambda i: (i // steps, 0, 0)),
                  pl.BlockSpec((1, 1, d), lambda i: (i // steps, 0, 0)),
                  pl.BlockSpec((d, n), lambda i: (0, 0), pipeline_mode=pl.Buffered(1))],
        out_specs=out_specs,
        out_shape=out_shape,
        compiler_params=_params(("parallel",)),
    )(x.reshape(b * t, d), g.reshape(1, d), sc.reshape(b, 1, d), sh.reshape(b, 1, d), w.astype(BF16))
    return [o.reshape(s // LANES, b, t, LANES) if i in blocked else o.reshape(b, t, s)
            for i, (o, s) in enumerate(zip(outs, splits))]


def _gls_chunks(groups, heads, cstack, sel, lmask, eye):
    ell = CHUNK
    c2 = jnp.concatenate([cstack, cstack], axis=1)
    es = []
    for i in range(0, len(groups), 2):
        rhs = jnp.concatenate([jnp.concatenate(_split(g), axis=0) for _, _, g in groups[i:i + 2]], axis=1)
        e2 = jnp.exp(_dot(c2, rhs))
        es += [e2[:, :LANES], e2[:, LANES:]]
    xs = [[e[(2 + l) * ell:(3 + l) * ell] * jnp.where(sel[l] > 0.0, q, k) for l in range(LEVELS)]
          for (q, k, _), e in zip(groups, es)]

    def msk(z, lm):
        return z if lm is None else z * lm

    if heads[0][2] is None:
        att = [mm1(groups[gi][0], groups[gi][1], NT) * eye for gi, _, _ in heads]
        for l in range(LEVELS):
            att = [a + mm1(xs[gi][l], xs[gi][l], NT) * lmask[l] for a, (gi, _, _) in zip(att, heads)]
        ov = [mm1(a, v) for a, (_, v, _) in zip(att, heads)]
    else:
        pairs = [(heads[i], heads[i + 1]) for i in range(0, len(heads), 2)]
        hm = [pairs[0][0][2], pairs[0][1][2]]
        two = lambda m: jnp.concatenate([m, m], axis=1)
        att = [_dot(groups[a[0]][0].astype(BF16), _bd(groups[a[0]][1], hm), NT) * two(eye) for a, _ in pairs]
        for l in range(LEVELS):
            lm2 = two(lmask[l])
            att = [t + _dot(xs[a[0]][l].astype(BF16), _bd(xs[a[0]][l], hm), NT) * lm2
                   for t, (a, _) in zip(att, pairs)]
        zv = jnp.zeros((ell, LANES), BF16)
        ov = []
        for t, (a, b) in zip(att, pairs):
            va, vb = a[1].astype(BF16), b[1].astype(BF16)
            vbd = jnp.concatenate([jnp.concatenate([va, zv], axis=1), jnp.concatenate([zv, vb], axis=1)], axis=0)
            o2 = _dot(t.astype(BF16), vbd)
            ov += [o2[:, :LANES], o2[:, LANES:]]
    qb = [msk(groups[gi][0] * es[gi][0:ell], lm) for gi, _, lm in heads]
    kv = [mm1(v, msk(groups[gi][1] * es[gi][ell:2 * ell], lm), TN) for gi, v, lm in heads]
    dec = [e[8 * ell:8 * ell + 1] for e in es]
    return ov, qb, kv, dec


def _gls_kernel(*refs, mode, rev, nsub, nchunk, cu):
    if mode == "hgrn":
        (q_ref, v_ref, f_ref, lb_ref, s0_ref, cst_ref, sel_ref, lm_ref, eye_ref,
         o_ref, sfin_ref, st_ref) = refs
    else:
        (q_ref, k_ref, v_ref, gd_ref, w2_ref, gb_ref, hm_ref, s0_ref, cst_ref, sel_ref, lm_ref, eye_ref,
         o_ref, sfin_ref, st_ref) = refs
    t = pl.program_id(1)
    ngroups = q_ref.shape[-1] // LANES

    @pl.when(t == 0)
    def _():
        st_ref[...] = s0_ref[0]

    cstack = cst_ref[...]
    sel = [sel_ref[l] for l in range(LEVELS)]
    lmask = [lm_ref[l] for l in range(LEVELS)]
    eye = eye_ref[...]

    def body(it, carry):
        nh = ngroups * nsub
        state = [st_ref[h] for h in range(nh)]
        rows, loaded = [], []
        for j in range(cu):
            ci = it * cu + j
            c = (nchunk - 1 - ci) if rev else ci
            rows.append(pl.ds(pl.multiple_of(c * CHUNK, CHUNK), CHUNK))
            if mode == "hgrn":
                loaded.append((q_ref[0, rows[j], :], f_ref[0, rows[j], :], v_ref[0, rows[j], :]))
            else:
                loaded.append((q_ref[0, rows[j], :], k_ref[0, rows[j], :], gd_ref[0, rows[j], :],
                               v_ref[0, rows[j], :]))
        groups, heads = [], []
        for j in range(cu):
            if mode == "hgrn":
                qx, fx, v_all = loaded[j]
                q_all = _silu(qx) * (HG_DK ** -0.5)
                lb = lb_ref[...]
                f = lb + (1.0 - lb) * jax.nn.sigmoid(fx)
                k_all = 1.0 - f
                g_all = jnp.log(f)
            else:
                qx, k_all, gdx, v_all = loaded[j]
                q_all = qx * (GLA_DK ** -0.5)
                z = mm1(gdx, w2_ref[...]) + gb_ref[...]
                g_all = (jnp.minimum(z, 0.0) - jnp.log1p(jnp.exp(-jnp.abs(z)))) * (1.0 / GLA_TAU)
            for gi in range(ngroups):
                ls = slice(gi * LANES, (gi + 1) * LANES)
                groups.append((q_all[:, ls], k_all[:, ls], g_all[:, ls]))
                for i in range(nsub):
                    h = gi * nsub + i
                    heads.append((j * ngroups + gi, v_all[:, h * LANES:(h + 1) * LANES],
                                  None if nsub == 1 else hm_ref[i:i + 1, :]))
        ov, qb, kv, dec = _gls_chunks(groups, heads, cstack, sel, lmask, eye)
        results = []
        for idx, (gidx, _, lm) in enumerate(heads):
            j, h = idx // nh, idx % nh
            results.append((j, h, ov[idx] + mm1(qb[idx], state[h], NT)))
            st_new = state[h] * dec[gidx] + kv[idx]
            state[h] = st_new if lm is None else st_new * lm
        for j, h, o in results:
            o_ref[0, rows[j], h * LANES:(h + 1) * LANES] = o
        for h in range(nh):
            st_ref[h] = state[h]
        return carry

    lax.fori_loop(0, nchunk // cu, body, 0)

    @pl.when(t == pl.num_programs(1) - 1)
    def _():
        sfin_ref[0] = st_ref[...]


def gated_scan(mode, rev, arrays, s0, tb=1024, cu=4):
    consts = _gls_consts(rev)
    if mode == "hgrn":
        p_hg, lb, d = arrays
        b, t, _ = p_hg.shape
        nh, nsub, width = HG_HEADS, 1, HG_WIDTH
    else:
        q, k, v, gd, w2pad, gb = arrays
        b, t, _ = q.shape
        nh, nsub, width = GLA_HEADS, 2, GLA_WIDTH
    tb = min(tb, t)
    nt = t // tb
    nchunk = tb // CHUNK
    cu = min(cu, nchunk)

    def tmap(i):
        return (nt - 1 - i) if rev else i

    def col(c, w):
        return pl.BlockSpec((1, tb, w), lambda bi, ti: (bi, tmap(ti), c))

    st_spec = pl.BlockSpec((1, nh, s0.shape[2], LANES), lambda bi, ti: (bi, 0, 0, 0))
    cspecs = [_full(c.shape) for c in consts]
    if mode == "hgrn":
        ins = [p_hg, p_hg, p_hg, lb, s0, *consts]
        in_specs = [col(0, width), col(1, width), col(2 + d, width), _full((1, width)), st_spec, *cspecs]
    else:
        hm = np.zeros((2, LANES), np.float32)
        hm[0, :GLA_DK] = 1.0
        hm[1, GLA_DK:] = 1.0
        ins = [q, k, v, gd, w2pad, gb, jnp.asarray(hm), s0, *consts]
        in_specs = [col(0, GLA_QK), col(0, GLA_QK), col(0, width), col(0, 2 * GLA_GATE_LORA),
                    _full(w2pad.shape), _full(gb.shape), _full((2, LANES)), st_spec, *cspecs]
    o, sfin = pl.pallas_call(
        functools.partial(_gls_kernel, mode=mode, rev=rev, nsub=nsub, nchunk=nchunk, cu=cu),
        grid=(b, nt),
        in_specs=in_specs,
        out_specs=[pl.BlockSpec((1, tb, width), lambda bi, ti: (bi, tmap(ti), 0)), st_spec],
        out_shape=[jax.ShapeDtypeStruct((b, t, width), F32), jax.ShapeDtypeStruct(s0.shape, F32)],
        scratch_shapes=[pltpu.VMEM(s0.shape[1:], F32)],
        compiler_params=_params(("parallel", "arbitrary")),
    )(*ins)
    return o, sfin


def _rwkv_prep_kernel(p_ref, pp_ref, pn_ref, mu_ref, w0_ref, w2_ref, a0_ref, a2_ref, g2_ref, kk_ref, ka_ref,
                      rk_ref, ones_ref, r_o, v_o, kk_o, gate_o, bonus_o, lw_o, kd_o, kb_o):
    t = pl.program_id(1)
    nt = pl.num_programs(1)
    p = p_ref[0]
    tb = p.shape[0]
    row = lax.broadcasted_iota(jnp.int32, (tb, 1), 0)
    prev_row = jnp.where(t > 0, pp_ref[0, 7:8, :], 0.0)
    next_row = jnp.where(t < nt - 1, pn_ref[0, 0:1, :], 0.0)
    prev = jnp.where(row == 0, prev_row, pltpu.roll(p, 1, axis=0))
    nxt = jnp.where(row == tb - 1, next_row, pltpu.roll(p, tb - 1, axis=0))
    s = p + mu_ref[0:1, :] * (prev - p) + mu_ref[1:2, :] * (nxt - p)
    w = RW_WIDTH
    r, k, v = s[:, 0:w], s[:, w:2 * w], s[:, 2 * w:3 * w]
    wd = jnp.tanh(s[:, 3 * w:3 * w + LANES])
    ad = s[:, 3 * w + LANES:3 * w + 2 * LANES]
    gd = s[:, 3 * w + 2 * LANES:3 * w + 3 * LANES]
    ones = ones_ref[...]
    kk = k * kk_ref[...]
    nrm = jnp.sqrt(mmcr(kk * kk, ones))
    kk = kk / jnp.maximum(nrm, 1e-12)
    gate = mm1(jax.nn.sigmoid(gd), g2_ref[...])
    ksum = jnp.zeros_like(k)
    for d in range(2):
        zw = w0_ref[d:d + 1, :] + mm1(wd, w2_ref[d])
        w_log = -(jnp.maximum(-zw, 0.0) + jnp.log1p(jnp.exp(-jnp.abs(zw)))) - 0.5
        lw = -jnp.exp(w_log)
        a = jax.nn.sigmoid(a0_ref[d:d + 1, :] + mm1(ad, a2_ref[d]))
        kd = k * (1.0 + (a - 1.0) * ka_ref[...])
        ksum = ksum + kd
        lw_o[d, 0] = lw
        kd_o[d, 0] = kd
        kb_o[d, 0] = kk * a
    r_o[0] = r
    v_o[0] = v
    kk_o[0] = kk
    gate_o[0] = gate
    bonus_o[0] = mmcr(r * ksum * rk_ref[...], ones) * v


def rwkv_prep(p_rw, mu, w0, w2, a0, a2, g2, k_k, k_a, r_k, tb=512):
    b, t, c = p_rw.shape
    tb = min(tb, t)
    nt = t // tb
    w = RW_WIDTH
    z = jnp.zeros((RW_DECAY_LORA, w), F32)
    w2p = jnp.stack([jnp.concatenate([w2[0], z], 0), jnp.concatenate([z, w2[1]], 0)]).astype(BF16)
    a2p = jnp.stack([jnp.concatenate([a2[0], z], 0), jnp.concatenate([z, a2[1]], 0)]).astype(BF16)
    ones = np.kron(np.eye(RW_HEADS, dtype=np.float32), np.ones((RW_DH, RW_DH), np.float32))
    nb8 = t // 8
    hm = jax.ShapeDtypeStruct((b, t, w), F32)
    hm2 = jax.ShapeDtypeStruct((2, b, t, w), F32)
    hm_spec = pl.BlockSpec((1, tb, w), lambda bi, ti: (bi, ti, 0))
    hm2_spec = pl.BlockSpec((2, 1, tb, w), lambda bi, ti: (0, bi, ti, 0))
    return pl.pallas_call(
        _rwkv_prep_kernel,
        grid=(b, nt),
        in_specs=[pl.BlockSpec((1, tb, c), lambda bi, ti: (bi, ti, 0)),
                  pl.BlockSpec((1, 8, c), lambda bi, ti: (bi, jnp.maximum(ti * (tb // 8) - 1, 0), 0)),
                  pl.BlockSpec((1, 8, c), lambda bi, ti: (bi, jnp.minimum((ti + 1) * (tb // 8), nb8 - 1), 0)),
                  _full((2, c)), _full((2, w)), _full((2, LANES, w)), _full((2, w)), _full((2, LANES, w)),
                  _full((RW_GATE_LORA, w)), _full((1, w)), _full((1, w)), _full((1, w)), _full((w, w))],
        out_specs=[hm_spec] * 5 + [hm2_spec] * 3,
        out_shape=[hm] * 5 + [hm2] * 3,
        compiler_params=_params(("parallel", "parallel")),
    )(p_rw, p_rw, p_rw, mu, w0, w2p, a0, a2p, g2.astype(BF16), k_k.reshape(1, w), k_a.reshape(1, w),
      r_k.reshape(1, w), jnp.asarray(ones, BF16))


def _bd(y, hm):
    return jnp.concatenate([y * hm[0], y * hm[1]], axis=0).astype(BF16)


def _bdiag(z, hm):
    return z[:CHUNK] * hm[0] + z[CHUNK:] * hm[1]


def _rwkv_chunks(ins, cum, tri, lmask, eye, hm):
    ell = CHUNK
    nc = range(len(ins))
    cum2 = jnp.concatenate([cum, cum], axis=1)
    cs = []
    for i in range(0, len(ins), 2):
        rhs = jnp.concatenate([jnp.concatenate(_split(x[5]), axis=0) for x in ins[i:i + 2]], axis=1)
        c2 = _dot(cum2, rhs)
        cs += [c2[:, :LANES], c2[:, LANES:]]
    rt = [ins[i][0] * jnp.exp(cs[i][0:ell]) for i in nc]
    at = [ins[i][3] * jnp.exp(cs[i][ell:2 * ell]) for i in nc]
    e_neg = [jnp.exp(-cs[i][0:ell]) for i in nc]
    e_rem = [jnp.exp(cs[i][2 * ell:3 * ell]) for i in nc]
    e_tot = [jnp.exp(cs[i][3 * ell:3 * ell + 1]) for i in nc]
    ar = [jnp.concatenate([at[i], rt[i]], axis=0).astype(BF16) for i in nc]
    pb = [_dot(ar[i], _bd(ins[i][4] * e_neg[i], hm), NT) for i in nc]
    pk = [_dot(ar[i], _bd(ins[i][1] * e_neg[i], hm), NT) for i in nc]
    n = [pb[i][:ell] * tri[0] for i in nc]
    ti = [eye + n[i] * lmask[0] for i in nc]
    for l in range(1, LEVELS):
        p = [_dot((n[i] * lmask[l]).astype(BF16), _bd(ti[i], hm)) for i in nc]
        ti = [ti[i] + _dot(ti[i].astype(BF16), _bd(p[i], hm)) for i in nc]
    vbd = [_bd(ins[i][2], hm) for i in nc]
    mv = [_dot((pk[i][:ell] * tri[0]).astype(BF16), vbd[i]) for i in nc]
    au = [_dot(ti[i].astype(BF16), jnp.concatenate([_bd(at[i], hm), _bd(mv[i], hm)], axis=1)) for i in nc]
    ro = [_dot((pb[i][ell:] * tri[1]).astype(BF16),
               jnp.concatenate([_bd(au[i][:, :LANES], hm), _bd(au[i][:, LANES:], hm)], axis=1)) for i in nc]
    rk = [_dot((pk[i][ell:] * tri[1]).astype(BF16), vbd[i]) for i in nc]
    gh = [_dot((ins[i][4] * e_rem[i]).astype(BF16), au[i].astype(BF16), TN) for i in nc]
    kv = [_dot((ins[i][1] * e_rem[i]).astype(BF16), ins[i][2].astype(BF16), TN) for i in nc]
    out = []
    for i in nc:
        rhat = rt[i] + ro[i][:, :LANES]
        ohat = ro[i][:, LANES:] + rk[i]
        g = eye * e_tot[i] + _bdiag(gh[i][:, :LANES], hm)
        h = _bdiag(gh[i][:, LANES:], hm) + _bdiag(kv[i], hm)
        out.append((rhat, ohat, g, h))
    return out


def _rwkv_state_mm(x, st, hm):
    s_hi = st.astype(BF16).astype(F32)
    bdh, bdl = _bd(s_hi, hm), _bd(st - s_hi, hm)
    xh, xl = _split(x)
    return _dot(jnp.concatenate([xh, xl], axis=1), jnp.concatenate([bdh, bdh], axis=0)) + _dot(xh, bdl)


def _rwkv_kernel(r_ref, v_ref, a_ref, lw_ref, k_ref, b_ref, s0_ref, cum_ref, tri_ref, lm_ref, eye_ref, hm_ref,
                 o_ref, sfin_ref, st_ref, *, rev, nchunk, cu):
    t = pl.program_id(1)
    npair = r_ref.shape[-1] // LANES

    @pl.when(t == 0)
    def _():
        st_ref[...] = s0_ref[0]

    cum = cum_ref[...]
    tri = [tri_ref[0], tri_ref[1]]
    lmask = [lm_ref[l] for l in range(LEVELS)]
    eye = eye_ref[...]
    hm = [hm_ref[0:1, :], hm_ref[1:2, :]]

    def body(it, carry):
        sts = [st_ref[p] for p in range(npair)]
        rows, ins = [], []
        for j in range(cu):
            ci = it * cu + j
            c = (nchunk - 1 - ci) if rev else ci
            rows.append(pl.ds(pl.multiple_of(c * CHUNK, CHUNK), CHUNK))
            for p in range(npair):
                ls = slice(p * LANES, (p + 1) * LANES)
                ins.append((r_ref[0, rows[j], ls], k_ref[0, 0, rows[j], ls], v_ref[0, rows[j], ls],
                            -a_ref[0, rows[j], ls], b_ref[0, 0, rows[j], ls], lw_ref[0, 0, rows[j], ls]))
        preps = _rwkv_chunks(ins, cum, tri, lmask, eye, hm)
        outs = []
        for j in range(cu):
            for p in range(npair):
                rhat, ohat, g, hh = preps[j * npair + p]
                res = _rwkv_state_mm(jnp.concatenate([rhat, g], axis=0), sts[p], hm)
                outs.append(res[:CHUNK] + ohat)
                sts[p] = res[CHUNK:] + hh
        for j in range(cu):
            for p in range(npair):
                o_ref[0, rows[j], p * LANES:(p + 1) * LANES] = outs[j * npair + p]
        for p in range(npair):
            st_ref[p] = sts[p]
        return carry

    lax.fori_loop(0, nchunk // cu, body, 0)

    @pl.when(t == pl.num_programs(1) - 1)
    def _():
        sfin_ref[0] = st_ref[...]


def rwkv_scan(rev, d, r, v, kk, lw, kd, kb, s0, tb=1024, cu=4):
    b, t, w = r.shape
    tb = min(tb, t)
    nt = t // tb
    nchunk = tb // CHUNK
    cu = min(cu, nchunk)
    consts = _rwkv_consts(rev)

    def tmap(i):
        return (nt - 1 - i) if rev else i

    spec = pl.BlockSpec((1, tb, w), lambda bi, ti: (bi, tmap(ti), 0))
    spec2 = pl.BlockSpec((1, 1, tb, w), lambda bi, ti: (d, bi, tmap(ti), 0))
    st_spec = pl.BlockSpec((1,) + s0.shape[1:], lambda bi, ti: (bi, 0, 0, 0))
    return pl.pallas_call(
        functools.partial(_rwkv_kernel, rev=rev, nchunk=nchunk, cu=cu),
        grid=(b, nt),
        in_specs=[spec, spec, spec, spec2, spec2, spec2, st_spec] + [_full(c.shape) for c in consts],
        out_specs=[spec, st_spec],
        out_shape=[jax.ShapeDtypeStruct(r.shape, F32), jax.ShapeDtypeStruct(s0.shape, F32)],
        scratch_shapes=[pltpu.VMEM(s0.shape[1:], F32)],
        compiler_params=_params(("parallel", "arbitrary")),
    )(r, v, kk, lw, kd, kb, s0, *consts)


def _even_out_kernel(x_ref, g1_ref, of_ref, ob_ref, gh_ref, ng_ref, rf_ref, rb_ref, bonus_ref, gate_ref,
                     lng_ref, lnb_ref, ones_ref, w_ref, o_ref):
    z = of_ref[0] + ob_ref[0]
    gh = gh_ref[0]
    acc = None
    for h in range(HG_HEADS):
        ls = slice(h * HG_DV, (h + 1) * HG_DV)
        zh = z[:, ls]
        yh = zh * lax.rsqrt(jnp.mean(zh * zh, axis=-1, keepdims=True) + EPS) * ng_ref[...] * _silu(gh[:, ls])
        part = mm1(yh, w_ref[ls, :])
        acc = part if acc is None else acc + part
    o = rf_ref[0] + rb_ref[0]
    ones = ones_ref[...]
    mean = mmcr(o, ones) * (1.0 / RW_DH)
    cen = o - mean
    var = mmcr(cen * cen, ones) * (1.0 / RW_DH)
    o = cen * lax.rsqrt(var + RW_GN_EPS) * lng_ref[...] + lnb_ref[...]
    o = (o + bonus_ref[0]) * gate_ref[0]
    acc = acc + mm1(o, w_ref[HG_WIDTH:, :])
    o_ref[0] = x_ref[0] + g1_ref[0] * acc


def even_out(x, g1, o_f, o_b, p_hg, hg_g, rw_f, rw_b, bonus, gate, ln_g, ln_b, w_out, tm=512):
    b, t, d = x.shape
    tm = min(tm, t)
    tok = lambda w: pl.BlockSpec((1, tm, w), lambda bi, ti: (bi, ti, 0))
    hm = tok(RW_WIDTH)
    ones = np.kron(np.eye(RW_HEADS, dtype=np.float32), np.ones((RW_DH, RW_DH), np.float32))
    return pl.pallas_call(
        _even_out_kernel,
        grid=(b, t // tm),
        in_specs=[tok(d), pl.BlockSpec((1, 1, d), lambda bi, ti: (bi, 0, 0)), tok(HG_WIDTH), tok(HG_WIDTH),
                  pl.BlockSpec((1, tm, HG_WIDTH), lambda bi, ti: (bi, ti, 4)), _full((1, HG_DV)),
                  hm, hm, hm, hm, _full((1, RW_WIDTH)), _full((1, RW_WIDTH)), _full(ones.shape),
                  pl.BlockSpec(w_out.shape, lambda bi, ti: (0, 0), pipeline_mode=pl.Buffered(1))],
        out_specs=tok(d),
        out_shape=jax.ShapeDtypeStruct(x.shape, F32),
        compiler_params=_params(("parallel", "parallel")),
    )(x, g1.reshape(b, 1, d), o_f, o_b, p_hg, hg_g.reshape(1, HG_DV), rw_f, rw_b, bonus, gate,
      ln_g.reshape(1, RW_WIDTH), ln_b.reshape(1, RW_WIDTH), jnp.asarray(ones, BF16), w_out.astype(BF16))


def _cpow_table(zr, zi, n):
    def step(c, _):
        cr, ci = c
        return (cr * zr - ci * zi, cr * zi + ci * zr), (cr, ci)
    (_, _), (pr, pi) = lax.scan(step, (jnp.ones_like(zr), jnp.zeros_like(zr)), None, length=n)
    return pr, pi


def _s5_params(a_re, a_im, log_dt, b_re, b_im, c_re, c_im, rev, nsteps):
    ell = S5_CHUNK
    dt = jnp.exp(log_dt)[:, None]
    mag = jnp.exp(a_re * dt)
    lr, li = mag * jnp.cos(a_im * dt), mag * jnp.sin(a_im * dt)
    den = a_re * a_re + a_im * a_im
    fr = ((lr - 1.0) * a_re + li * a_im) / den
    fi = (li * a_re - (lr - 1.0) * a_im) / den
    bbr = fr[..., None] * b_re - fi[..., None] * b_im
    bbi = fr[..., None] * b_im + fi[..., None] * b_re
    pr, pi = _cpow_table(lr, li, ell + 1)
    pr_t, pi_t = jnp.transpose(pr, (1, 2, 0)), jnp.transpose(pi, (1, 2, 0))
    cre_t, cim_t = jnp.swapaxes(c_re, 1, 2), jnp.swapaxes(c_im, 1, 2)
    cr_p = cre_t[:, :, None, :] * pr_t[..., None] - cim_t[:, :, None, :] * pi_t[..., None]
    ci_p = cre_t[:, :, None, :] * pi_t[..., None] + cim_t[:, :, None, :] * pr_t[..., None]
    kern = (jnp.einsum('gntc,gni->gitc', cr_p[:, :, :ell], bbr)
            - jnp.einsum('gntc,gni->gitc', ci_p[:, :, :ell], bbi))
    s_idx = np.arange(ell)[:, None]
    t_idx = np.arange(ell)[None, :]
    lag = (s_idx - t_idx) if rev else (t_idx - s_idx)
    valid = jnp.asarray(lag >= 0)
    kt = jnp.take(kern, np.clip(lag, 0, ell - 1).reshape(-1), axis=2)
    kt = jnp.where(valid[None, None, :, :, None], kt.reshape(S5_GROUPS, S5_GROUP, ell, ell, S5_GROUP), 0.0)
    toep = jnp.transpose(kt, (0, 2, 1, 3, 4)).reshape(S5_GROUPS, S5_FEAT, S5_FEAT)
    e_idx = (np.arange(ell) if rev else (ell - 1 - np.arange(ell)))
    pwr, pwi = jnp.swapaxes(pr[e_idx], 0, 1), jnp.swapaxes(pi[e_idx], 0, 1)
    bbr_t, bbi_t = jnp.swapaxes(bbr, 1, 2), jnp.swapaxes(bbi, 1, 2)
    p_re = pwr[:, :, None, :] * bbr_t[:, None] - pwi[:, :, None, :] * bbi_t[:, None]
    p_im = pwr[:, :, None, :] * bbi_t[:, None] + pwi[:, :, None, :] * bbr_t[:, None]
    p_re = p_re.reshape(S5_GROUPS, S5_FEAT, S5_STATE)
    p_im = p_im.reshape(S5_GROUPS, S5_FEAT, S5_STATE)
    q_sl = slice(ell, 0, -1) if rev else slice(1, ell + 1)
    q_re = cr_p[:, :, q_sl].reshape(S5_GROUPS, S5_STATE, S5_FEAT)
    q_im = -ci_p[:, :, q_sl].reshape(S5_GROUPS, S5_STATE, S5_FEAT)
    zr, zi = [pr[ell]], [pi[ell]]
    for _ in range(nsteps - 1):
        zr, zi = zr + [zr[-1] * zr[-1] - zi[-1] * zi[-1]], zi + [2.0 * zr[-1] * zi[-1]]
    zr = jnp.stack(zr)[:, :, None, :]
    zi = jnp.stack(zi)[:, :, None, :]
    return toep, p_re, p_im, q_re, q_im, zr, zi


def _s5_expand(toep, p_re, p_im, q_re, q_im, zr, zi):
    ng = LANES // S5_GROUP
    nc = S5_GROUPS // ng
    ell, grp, st = S5_CHUNK, S5_GROUP, S5_STATE
    eye = jnp.eye(ng, dtype=BF16)
    kw = ell * LANES
    w = None if toep is None else (
        toep.astype(BF16).reshape(nc, ng, ell * grp, ell, 1, grp) * eye[None, :, None, None, :, None]
    ).reshape(nc, kw, kw)
    pe = lambda p: (p.astype(BF16).reshape(nc, ng, ell * grp, 1, st) * eye[None, :, None, :, None]
                    ).reshape(nc, kw, ng * st)
    qe = lambda q: (q.astype(BF16).reshape(nc, ng, st, ell, 1, grp) * eye[None, :, None, None, :, None]
                    ).reshape(nc, ng * st, kw)
    p = jnp.concatenate([pe(p_re), pe(p_im)], axis=2)
    q = jnp.concatenate([qe(q_re), qe(q_im)], axis=1)
    ns = zr.shape[0]
    return w, p, q, zr.reshape(ns, nc, 1, ng * st), zi.reshape(ns, nc, 1, ng * st)


def _s5_kernel(*refs, rev, nsteps, intra):
    if intra:
        u_ref, w_ref, p_ref, q_ref, zr_ref, zi_ref, h0r_ref, h0i_ref, y_ref, hfr_ref, hfi_ref, cr_ref, ci_ref = refs
    else:
        u_ref, p_ref, q_ref, zr_ref, zi_ref, h0r_ref, h0i_ref, y_ref, hfr_ref, hfi_ref, cr_ref, ci_ref = refs
    t = pl.program_id(2)

    @pl.when(t == 0)
    def _():
        cr_ref[...] = h0r_ref[0, 0]
        ci_ref[...] = h0i_ref[0, 0]

    ng = LANES // S5_GROUP
    jb = u_ref.shape[2] // S5_CHUNK
    pieces = [u_ref[0, 0, pl.ds(s, jb, stride=S5_CHUNK), :] for s in range(S5_CHUNK)]
    slot = lax.broadcasted_iota(jnp.int32, (1, LANES), 1) // S5_GROUP
    cols = []
    for g in range(ng):
        for h in range(S5_CHUNK // ng):
            acc = None
            for sp in range(ng):
                shift = ((sp - g) * S5_GROUP) % LANES
                piece = pieces[h * ng + sp]
                rolled = piece if shift == 0 else pltpu.roll(piece, shift, axis=1)
                acc = rolled if acc is None else jnp.where(slot == sp, rolled, acc)
            cols.append(acc.astype(BF16))
    u = jnp.concatenate(cols, axis=1)
    half = p_ref.shape[-1] // 2
    row = lax.broadcasted_iota(jnp.int32, (jb, 1), 0)
    edge = (jb - 1) if rev else 0
    x = _dot(u, p_ref[0])
    xr, xi = x[:, :half], x[:, half:]
    car_r, car_i = cr_ref[...], ci_ref[...]
    z1r, z1i = zr_ref[0, 0], zi_ref[0, 0]
    at_edge = row == edge
    hr = xr + jnp.where(at_edge, z1r * car_r - z1i * car_i, 0.0)
    hi = xi + jnp.where(at_edge, z1r * car_i + z1i * car_r, 0.0)
    for s in range(nsteps):
        sh = 1 << s
        if sh >= jb:
            break
        zr, zi = zr_ref[s, 0], zi_ref[s, 0]
        if rev:
            pr_ = jnp.where(row < jb - sh, pltpu.roll(hr, jb - sh, axis=0), 0.0)
            pi_ = jnp.where(row < jb - sh, pltpu.roll(hi, jb - sh, axis=0), 0.0)
        else:
            pr_ = jnp.where(row >= sh, pltpu.roll(hr, sh, axis=0), 0.0)
            pi_ = jnp.where(row >= sh, pltpu.roll(hi, sh, axis=0), 0.0)
        hr, hi = hr + (zr * pr_ - zi * pi_), hi + (zr * pi_ + zi * pr_)
    if rev:
        hpr = jnp.where(at_edge, car_r, pltpu.roll(hr, jb - 1, axis=0))
        hpi = jnp.where(at_edge, car_i, pltpu.roll(hi, jb - 1, axis=0))
        cr_ref[...] = hr[0:1]
        ci_ref[...] = hi[0:1]
    else:
        hpr = jnp.where(at_edge, car_r, pltpu.roll(hr, 1, axis=0))
        hpi = jnp.where(at_edge, car_i, pltpu.roll(hi, 1, axis=0))
        cr_ref[...] = hr[jb - 1:jb]
        ci_ref[...] = hi[jb - 1:jb]
    y = _dot(jnp.concatenate([hpr, hpi], axis=1).astype(BF16), q_ref[0])
    if intra:
        y = y + _dot(u, w_ref[0])
    for s in range(S5_CHUNK):
        y_ref[0, 0, pl.ds(s, jb, stride=S5_CHUNK), :] = y[:, s * LANES:(s + 1) * LANES]

    @pl.when(t == pl.num_programs(2) - 1)
    def _():
        hfr_ref[0, 0] = cr_ref[...]
        hfi_ref[0, 0] = ci_ref[...]


def s5_dir(rev, u4, w, p, q, zr, zi, h0r, h0i, jb=S5_ROWS):
    nc, nb, t, _ = u4.shape
    j = t // S5_CHUNK
    jb = min(jb, j)
    nj = j // jb
    nsteps = max(int(math.log2(jb)), 1)
    nst = h0r.shape[-1]

    def tmap(ti):
        return (nj - 1 - ti) if rev else ti

    const = lambda a: pl.BlockSpec((1,) + a.shape[1:], lambda ci, bi, ti: (ci, 0, 0), pipeline_mode=pl.Buffered(1))
    hspec = pl.BlockSpec((1, 1, 1, nst), lambda ci, bi, ti: (ci, bi, 0, 0))
    zspec = pl.BlockSpec((zr.shape[0], 1, 1, nst), lambda ci, bi, ti: (0, ci, 0, 0))
    uspec = pl.BlockSpec((1, 1, jb * S5_CHUNK, LANES), lambda ci, bi, ti: (ci, bi, tmap(ti), 0))
    mats = ([w] if w is not None else []) + [p, q]
    y, hfr, hfi = pl.pallas_call(
        functools.partial(_s5_kernel, rev=rev, nsteps=nsteps, intra=w is not None),
        grid=(nc, nb, nj),
        in_specs=[uspec] + [const(m) for m in mats] + [zspec, zspec, hspec, hspec],
        out_specs=[uspec, hspec, hspec],
        out_shape=[jax.ShapeDtypeStruct(u4.shape, F32), jax.ShapeDtypeStruct(h0r.shape, F32),
                   jax.ShapeDtypeStruct(h0i.shape, F32)],
        scratch_shapes=[pltpu.VMEM((1, nst), F32), pltpu.VMEM((1, nst), F32)],
        compiler_params=_params(("parallel", "parallel", "arbitrary")),
    )(u4, *mats, zr, zi, h0r, h0i)
    return y, hfr, hfi


def _gelu_tanh(x):
    return 0.5 * x * (1.0 + jnp.tanh(math.sqrt(2.0 / math.pi) * (x + 0.044715 * (x * x * x))))


def _odd_out_kernel(x_ref, g1_ref, of_ref, ob_ref, gg_ref, ng_ref, yf_ref, yb_ref, u_ref, dsk_ref,
                    wglu_ref, bglu_ref, w_ref, o_ref):
    z = of_ref[0] + ob_ref[0]
    gg = gg_ref[0]
    acc = None
    for h in range(GLA_HEADS):
        ls = slice(h * GLA_DV, (h + 1) * GLA_DV)
        zh = z[:, ls]
        yh = zh * lax.rsqrt(jnp.mean(zh * zh, axis=-1, keepdims=True) + EPS) * ng_ref[...] * _silu(gg[:, ls])
        part = mm1(yh, w_ref[ls, :])
        acc = part if acc is None else acc + part
    cat = lambda r: jnp.concatenate([r[c, 0] for c in range(r.shape[0])], axis=1)
    y = (cat(yf_ref) + cat(yb_ref)) + dsk_ref[...] * cat(u_ref)
    zz = _gelu_tanh(y)
    o_s5 = zz * jax.nn.sigmoid(mm1(zz, wglu_ref[...]) + bglu_ref[...])
    acc = acc + mm1(o_s5, w_ref[GLA_WIDTH:, :])
    o_ref[0] = x_ref[0] + g1_ref[0] * acc


def odd_out(x, g1, o_f, o_b, g_gla, gla_g, y_f, y_b, u, d_skip, w_glu, b_glu, w_out, tm=512):
    b, t, d = x.shape
    tm = min(tm, t)
    tok = lambda w: pl.BlockSpec((1, tm, w), lambda bi, ti: (bi, ti, 0))
    const = lambda a: pl.BlockSpec(a.shape, lambda bi, ti: (0, 0), pipeline_mode=pl.Buffered(1))
    blk = pl.BlockSpec((S5_WIDTH // LANES, 1, tm, LANES), lambda bi, ti: (0, bi, ti, 0))
    return pl.pallas_call(
        _odd_out_kernel,
        grid=(b, t // tm),
        in_specs=[tok(d), pl.BlockSpec((1, 1, d), lambda bi, ti: (bi, 0, 0)), tok(GLA_WIDTH), tok(GLA_WIDTH),
                  tok(GLA_WIDTH), _full((1, GLA_DV)), blk, blk, blk,
                  _full((1, S5_WIDTH)), const(w_glu), _full((1, S5_WIDTH)), const(w_out)],
        out_specs=tok(d),
        out_shape=jax.ShapeDtypeStruct(x.shape, F32),
        compiler_params=_params(("parallel", "parallel")),
    )(x, g1.reshape(b, 1, d), o_f, o_b, g_gla, gla_g.reshape(1, GLA_DV), y_f, y_b, u,
      d_skip.reshape(1, S5_WIDTH), w_glu.astype(BF16), b_glu.reshape(1, S5_WIDTH), w_out.astype(BF16))


def _ffn_kernel(x_ref, xp_ref, xn_ref, ng_ref, sc_ref, sh_ref, g2_ref, wup_ref, cw_ref, cb_ref, wdn_ref,
                fg_ref, o_ref, *, cols, fw, ahead, final):
    r = pl.program_id(1)
    nr = pl.num_programs(1)
    x = x_ref[0]
    n = x.shape[0]

    def modnorm(z):
        h = z * lax.rsqrt(jnp.mean(z * z, axis=-1, keepdims=True) + EPS) * ng_ref[...]
        return (h * (1.0 + sc_ref[0]) + sh_ref[0]).astype(BF16)

    h_main = modnorm(x)
    h_prev = jnp.where(r > 0, modnorm(xp_ref[0]), jnp.zeros((), BF16))
    h_next = jnp.where(r < nr - 1, modnorm(xn_ref[0]), jnp.zeros((), BF16))
    h_all = jnp.concatenate([h_prev, h_main, h_next], axis=0)
    pos = lax.broadcasted_iota(jnp.int32, (n, 1), 0)
    colid = jnp.bitwise_and(pos, cols - 1)
    first_col, last_col = colid == 0, colid == cols - 1
    slices = list(range(0, D_FF, fw))

    def up(f0):
        return (_dot(h_all, wup_ref[:, f0:f0 + fw]),
                _dot(h_main, wup_ref[:, D_FF + f0:D_FF + f0 + fw]))

    acc = None
    ups = [up(f0) for f0 in slices[:ahead]]
    for i, f0 in enumerate(slices):
        gate, val = ups[i]
        if i + ahead < len(slices):
            ups.append(up(slices[i + ahead]))
        cw = lambda tap: cw_ref[tap:tap + 1, f0:f0 + fw]
        rows3 = [gate[di * cols:di * cols + n] for di in range(3)]
        left = rows3[0] * cw(0) + rows3[1] * cw(3) + rows3[2] * cw(6)
        mid = rows3[0] * cw(1) + rows3[1] * cw(4) + rows3[2] * cw(7)
        right = rows3[0] * cw(2) + rows3[1] * cw(5) + rows3[2] * cw(8)
        conv = (cb_ref[:, f0:f0 + fw] + mid
                + jnp.where(first_col, 0.0, pltpu.roll(left, 1, axis=0))
                + jnp.where(last_col, 0.0, pltpu.roll(right, n - 1, axis=0)))
        act = (_silu(conv) * val).astype(BF16)
        part = _dot(act, wdn_ref[f0:f0 + fw, :])
        acc = part if acc is None else acc + part
    y = x + g2_ref[0] * acc
    if final:
        y = y * lax.rsqrt(jnp.mean(y * y, axis=-1, keepdims=True) + EPS) * fg_ref[...]
    o_ref[0] = y


def conv_ffn(x, ng, sc, sh, g2, w_up, conv_w, conv_b, w_down, final_g, rows, cols, final, rt=16, fw=256,
             ahead=3):
    b, t, d = x.shape
    rt = min(rt, rows)
    nr = rows // rt
    n = rt * cols
    vec = lambda a: pl.BlockSpec((1, 1, d), lambda bi, ri: (bi, 0, 0))
    const = lambda a: pl.BlockSpec(a.shape, lambda bi, ri: (0, 0), pipeline_mode=pl.Buffered(1))
    wup = w_up.astype(BF16)
    wdn = w_down.astype(BF16)
    cw = conv_w.reshape(9, D_FF)
    cb = conv_b.reshape(1, D_FF)
    return pl.pallas_call(
        functools.partial(_ffn_kernel, cols=cols, fw=fw, ahead=ahead, final=final),
        grid=(b, nr),
        in_specs=[pl.BlockSpec((1, n, d), lambda bi, ri: (bi, ri, 0)),
                  pl.BlockSpec((1, cols, d), lambda bi, ri: (bi, jnp.maximum(ri * rt - 1, 0), 0)),
                  pl.BlockSpec((1, cols, d), lambda bi, ri: (bi, jnp.minimum((ri + 1) * rt, rows - 1), 0)),
                  _full((1, d)), vec(sc), vec(sh), vec(g2), const(wup), _full((9, D_FF)), _full((1, D_FF)),
                  const(wdn), _full((1, d))],
        out_specs=pl.BlockSpec((1, n, d), lambda bi, ri: (bi, ri, 0)),
        out_shape=jax.ShapeDtypeStruct(x.shape, F32),
        compiler_params=_params(("parallel", "parallel")),
    )(x, x, x, ng.reshape(1, d), sc.reshape(b, 1, d), sh.reshape(b, 1, d), g2.reshape(b, 1, d), wup, cw, cb,
      wdn, final_g.reshape(1, d))


def _even_layer(hx_args, ctx_args, prm, with_ctx_out):
    (w_in, w_out, lb, hg_g, mu, w0, w2, a0, a2, g2, k_k, k_a, r_k, ln_g, ln_b) = prm
    hg_cols = 5 * HG_WIDTH
    res = {}
    bsz = hx_args[0].shape[0]
    s_hg = [jnp.zeros((bsz, HG_HEADS, HG_DV, LANES), F32)] * 2
    s_rw = [jnp.zeros((bsz, RW_WIDTH // LANES, RW_DH, LANES), F32)] * 2
    for name, (x, g, sc, sh, g1), want_out in (("ctx", ctx_args, with_ctx_out), ("x", hx_args, True)):
        p_hg, p_rw = norm_proj(x, g, sc, sh, w_in, (hg_cols, RW_COLS))
        r, v, kk, gate, bonus, lw, kd, kb = rwkv_prep(p_rw, mu, w0, w2, a0, a2, g2, k_k, k_a, r_k)
        o_hg, o_rw = [], []
        for d in range(2):
            o, s_hg[d] = gated_scan("hgrn", d == 1, (p_hg, lb[d:d + 1], d), s_hg[d])
            o_hg.append(o)
            o, s_rw[d] = rwkv_scan(d == 1, d, r, v, kk, lw, kd, kb, s_rw[d])
            o_rw.append(o)
        if want_out:
            res[name] = even_out(x, g1, o_hg[0], o_hg[1], p_hg, hg_g, o_rw[0], o_rw[1], bonus, gate,
                                 ln_g, ln_b, w_out)
    return res


def _odd_layer(hx_args, ctx_args, prm, with_ctx_out):
    (w_in, w_out, gla_w2, gla_b, gla_g, a_re, a_im, log_dt, b_re, b_im, c_re, c_im, d_skip, w_glu, b_glu) = prm
    nq = 2 * GLA_QK + 2 * GLA_WIDTH
    w_perm = jnp.concatenate([w_in[:, :nq], w_in[:, nq + 2 * GLA_GATE_LORA:], w_in[:, nq:nq + 2 * GLA_GATE_LORA]], 1)
    z16 = jnp.zeros((GLA_GATE_LORA, GLA_QK), F32)
    w2pad = [jnp.concatenate([gla_w2[0], z16], 0).astype(BF16), jnp.concatenate([z16, gla_w2[1]], 0).astype(BF16)]
    res = {}
    bsz = hx_args[0].shape[0]
    s_gla = [jnp.zeros((bsz, GLA_HEADS, GLA_DV, LANES), F32)] * 2
    zs = jnp.zeros((S5_WIDTH // LANES, bsz, 1, (LANES // S5_GROUP) * S5_STATE), F32)
    s_s5 = [(zs, zs), (zs, zs)]
    s5_mats = []
    for d in range(2):
        toep, *rest = _s5_params(a_re[d], a_im[d], log_dt[d], b_re[d], b_im[d], c_re[d], c_im[d],
                                 rev=d == 1, nsteps=S5_MAX_STEPS)
        s5_mats.append((toep, *rest))
    s5_mats = [_s5_expand(s5_mats[0][0] + s5_mats[1][0], *s5_mats[0][1:]), _s5_expand(None, *s5_mats[1][1:])]
    for name, (x, g, sc, sh, g1), want_out in (("ctx", ctx_args, with_ctx_out), ("x", hx_args, True)):
        q, k, v, gg, u, gd = norm_proj(x, g, sc, sh, w_perm,
                                       (GLA_QK, GLA_QK, GLA_WIDTH, GLA_WIDTH, S5_WIDTH, 2 * GLA_GATE_LORA),
                                       blocked=(4,))
        o_gla, y_s5 = [], []
        for d in range(2):
            o, s_gla[d] = gated_scan("gla", d == 1, (q, k, v, gd, w2pad[d], gla_b[d].reshape(1, GLA_QK)), s_gla[d],
                                     cu=4)
            o_gla.append(o)
            w_d, p_d, q_d, zr_d, zi_d = s5_mats[d]
            y, hr, hi = s5_dir(d == 1, u, w_d if d == 0 else None, p_d, q_d, zr_d, zi_d, s_s5[d][0], s_s5[d][1])
            s_s5[d] = (hr, hi)
            y_s5.append(y)
        if want_out:
            res[name] = odd_out(x, g1, o_gla[0], o_gla[1], gg, gla_g, y_s5[0], y_s5[1], u, d_skip, w_glu, b_glu,
                                w_out)
    return res


def kernel(x, c, ctx, c_ctx, ada_w, ada_b, norm1_g, norm2_g, final_g,
           ev_w_in, ev_w_out, hg_lb, hg_norm_g, rw_mu, rw_w0, rw_w2, rw_a0, rw_a2, rw_g2,
           rw_k_k, rw_k_a, rw_r_k, rw_ln_g, rw_ln_b,
           od_w_in, od_w_out, gla_w2, gla_b, gla_norm_g, s5_a_re, s5_a_im, s5_log_dt,
           s5_b_re, s5_b_im, s5_c_re, s5_c_im, s5_d, s5_w_glu, s5_b_glu,
           ffn_w_up, ffn_conv_w, ffn_conv_b, ffn_w_down):
    bsz, seq, d = x.shape
    depth = ada_w.shape[0]
    rows = seq // GRID_W
    ctx_len = ctx.shape[1]
    lb_all = jnp.cumsum(jax.nn.softmax(hg_lb.astype(F32), axis=0), axis=0)

    for layer in range(depth):
        last = layer == depth - 1
        j = layer // 2
        mod = jnp.split(jax.nn.silu(c) @ ada_w[layer] + ada_b[layer], 6, axis=-1)
        mod_c = jnp.split(jax.nn.silu(c_ctx) @ ada_w[layer] + ada_b[layer], 6, axis=-1)
        sh1, sc1, g1, sh2, sc2, g2 = mod
        csh1, csc1, cg1, csh2, csc2, cg2 = [jnp.broadcast_to(m[None], (bsz, d)) for m in mod_c]
        hx_args = (x, norm1_g[layer], sc1, sh1, g1)
        ctx_args = (ctx, norm1_g[layer], csc1, csh1, cg1)
        if layer % 2 == 0:
            prm = (ev_w_in[j], ev_w_out[j], lb_all[j], hg_norm_g[j], rw_mu[j], rw_w0[j], rw_w2[j], rw_a0[j],
                   rw_a2[j], rw_g2[j], rw_k_k[j], rw_k_a[j], rw_r_k[j], rw_ln_g[j], rw_ln_b[j])
            res = _even_layer(hx_args, ctx_args, prm, not last)
        else:
            prm = (od_w_in[j], od_w_out[j], gla_w2[j], gla_b[j], gla_norm_g[j], s5_a_re[j], s5_a_im[j],
                   s5_log_dt[j], s5_b_re[j], s5_b_im[j], s5_c_re[j], s5_c_im[j], s5_d[j], s5_w_glu[j], s5_b_glu[j])
            res = _odd_layer(hx_args, ctx_args, prm, not last)
        ffn = (ffn_w_up[layer], ffn_conv_w[layer], ffn_conv_b[layer], ffn_w_down[layer], final_g)
        x = conv_ffn(res["x"], norm2_g[layer], sc2, sh2, g2, *ffn, rows, GRID_W, final=last)
        if not last:
            ctx = conv_ffn(res["ctx"], norm2_g[layer], csc2, csh2, cg2, *ffn, 1, ctx_len, final=False)
    return x
```
